```python
import jax, jax.numpy as jnp
from jax import lax
import numpy as np

D_MODEL = 1024
BATCH = 1
SEQ = 16384
DEPTH = 1
DEC_BATCH = 128
DEC_SEQ = 4
PAST_LEN = 8192
PAGE_SIZE = 128

RW_HEADS = 8
RW_HEAD_DIM = 64
RW_WIDTH = RW_HEADS * RW_HEAD_DIM
DECAY_LORA = 64
A_LORA = 64
SB_HEADS = 8
SB_HEAD_DIM = 64
SB_WIDTH = SB_HEADS * SB_HEAD_DIM
Q_BLOCK = 128
SB_BIAS_INIT = -8.0
RW_COLS = 4 * RW_WIDTH + DECAY_LORA + A_LORA
SB_COLS = 4 * SB_WIDTH
IN_COLS = RW_COLS + SB_COLS + 2 * D_MODEL
RMS_EPS = 1e-6
GN_EPS = 64e-5
NORM_EPS = 1e-12

kernel_name = 'rwkv7_stickbreak_gated_hybrid_step'


def rms_norm(x, gain):
    x32 = x.astype(jnp.float32)
    y = x32 * lax.rsqrt(jnp.mean(x32 * x32, axis=-1, keepdims=True) + RMS_EPS)
    return (y * gain.astype(jnp.float32)).astype(x.dtype)


def project_inputs(x, c, w_ada, b_ada, norm_pre, w_in):
    mod = jax.nn.silu(c) @ w_ada + b_ada
    shift, scale, gate = jnp.split(mod[:, None, :], 3, axis=-1)
    h = rms_norm(x, norm_pre) * (1 + scale) + shift
    p = h @ w_in
    p_rw, p_sb, g_rw, g_sb = jnp.split(
        p, [RW_COLS, RW_COLS + SB_COLS, RW_COLS + SB_COLS + D_MODEL], axis=-1)
    return p_rw, p_sb, g_rw, g_sb, gate


def rwkv_recurrence(s0, r, decay, k, v, a_vec, b_vec):
    def step(s, inp):
        r_t, w_t, k_t, v_t, a_t, b_t = inp
        sa = jnp.einsum('bhvk,bhk->bhv', s, a_t)
        s = (s * w_t[:, :, None, :] + sa[..., None] * b_t[:, :, None, :]
             + v_t[..., None] * k_t[:, :, None, :])
        return s, jnp.einsum('bhvk,bhk->bhv', s, r_t)
    seq = tuple(jnp.moveaxis(t, 1, 0) for t in (r, decay, k, v, a_vec, b_vec))
    s_final, y = lax.scan(step, s0, seq)
    return jnp.moveaxis(y, 0, 1), s_final


def rwkv_branch(p, prev_row, s0, mu_shift, w0_decay, w_decay_up, a0, w_a_up,
                k_k, k_a, r_k, ln_x_w, ln_x_b):
    B, T, _ = p.shape
    f32 = jnp.float32
    prev = jnp.concatenate([prev_row[:, None, :].astype(p.dtype), p[:, :-1]], axis=1)
    m = p + (prev - p) * mu_shift
    r, k, v, z, w_lo, a_lo = jnp.split(
        m, [RW_WIDTH, 2 * RW_WIDTH, 3 * RW_WIDTH, 4 * RW_WIDTH, 4 * RW_WIDTH + DECAY_LORA], axis=-1)
    heads = lambda t: t.astype(f32).reshape(B, T, RW_HEADS, RW_HEAD_DIM)
    w_log = -jax.nn.softplus(-(w0_decay + jnp.tanh(w_lo) @ w_decay_up)) - 0.5
    a = jax.nn.sigmoid(a0 + a_lo @ w_a_up)
    kk = heads(k * k_k)
    kk = kk / jnp.maximum(jnp.sqrt(jnp.sum(kk * kk, axis=-1, keepdims=True)), NORM_EPS)
    k = k * (1 + (a - 1) * k_a)
    r_h, k_h, v_h, a_h = heads(r), heads(k), heads(v), heads(a)
    decay = jnp.exp(-jnp.exp(heads(w_log)))
    y, s_final = rwkv_recurrence(s0.astype(f32), r_h, decay, k_h, v_h, -kk, kk * a_h)
    mean = jnp.mean(y, axis=-1, keepdims=True)
    var = jnp.mean(jnp.square(y - mean), axis=-1, keepdims=True)
    y = ((y - mean) * lax.rsqrt(var + GN_EPS) * ln_x_w.astype(f32).reshape(RW_HEADS, RW_HEAD_DIM)
         + ln_x_b.astype(f32).reshape(RW_HEADS, RW_HEAD_DIM))
    y = y + jnp.sum(r_h * k_h * r_k.astype(f32), axis=-1, keepdims=True) * v_h
    o = y.reshape(B, T, RW_WIDTH).astype(p.dtype) * jax.nn.silu(z)
    return o, s_final, p[:, -1]


def sb_split(p_sb):
    B, T, _ = p_sb.shape
    q, k, v, z = jnp.split(p_sb, 4, axis=-1)
    hd = lambda t: t.reshape(B, T, SB_HEADS, SB_HEAD_DIM)
    return hd(q), hd(k), hd(v), z


def sb_weights(z, sb_bias, q_pos, k_pos):
    z = z.astype(jnp.float32) + sb_bias.astype(jnp.float32)[:, None, None]
    causal = k_pos[None, :] < q_pos[:, None]
    log_fail = jnp.where(causal, jax.nn.log_sigmoid(-z), 0.0)
    shifted = jnp.concatenate([log_fail[..., 1:], jnp.zeros_like(log_fail[..., :1])], axis=-1)
    rest = lax.cumsum(shifted, axis=z.ndim - 1, reverse=True)
    return jnp.where(causal, jnp.exp(jax.nn.log_sigmoid(z) + rest), 0.0)


def sb_prompt(q, k, v, sb_bias):
    B, S, H, D = q.shape
    nb = S // Q_BLOCK
    scale = SB_HEAD_DIM ** -0.5
    qb = jnp.moveaxis(q.reshape(B, nb, Q_BLOCK, H, D), 1, 0)
    k_pos = jnp.arange(S)

    def block(args):
        q_i, i = args
        z = jnp.einsum('bqhd,bkhd->bhqk', q_i, k) * scale
        a_w = sb_weights(z, sb_bias, i * Q_BLOCK + jnp.arange(Q_BLOCK), k_pos)
        return jnp.einsum('bhqk,bkhd->bqhd', a_w.astype(v.dtype), v)

    o = lax.map(block, (qb, jnp.arange(nb)))
    return jnp.moveaxis(o, 0, 1).reshape(B, S, H, D)


def sb_decode(q, k_new, v_new, cache_k, cache_v, page_table, sb_bias):
    B, T = q.shape[:2]
    past = page_table.shape[1] * cache_k.shape[1]
    scale = SB_HEAD_DIM ** -0.5
    k_past = cache_k[page_table].reshape(B, past, SB_HEADS, SB_HEAD_DIM)
    v_past = cache_v[page_table].reshape(B, past, SB_HEADS, SB_HEAD_DIM)
    z = jnp.concatenate([jnp.einsum('bqhd,bkhd->bhqk', q, k_past.astype(q.dtype)),
                         jnp.einsum('bqhd,bkhd->bhqk', q, k_new)], axis=-1) * scale
    a_w = sb_weights(z, sb_bias, past + jnp.arange(T), jnp.arange(past + T)).astype(v_new.dtype)
    return (jnp.einsum('bhqk,bkhd->bqhd', a_w[..., :past], v_past.astype(v_new.dtype))
            + jnp.einsum('bhqk,bkhd->bqhd', a_w[..., past:], v_new))


def merge_output(x, o_rw, o_sb, g_rw, g_sb, gate, w_branch_rwkv, w_branch_sb, w_out, norm_post):
    m = jax.nn.sigmoid(g_rw) * (o_rw @ w_branch_rwkv) + jax.nn.sigmoid(g_sb) * (o_sb @ w_branch_sb)
    return x + gate * rms_norm(m @ w_out, norm_post)


def setup_inputs(seed: int = 0) -> dict:
    key = jax.random.key(seed)
    ks = jax.random.split(key, 32)
    n_pages = PAST_LEN // PAGE_SIZE
    n_used = DEC_BATCH * n_pages
    n_pool = (n_used * 5) // 4
    nrm = lambda k, shape, s: jax.random.normal(k, shape, jnp.float32) * s
    uni = lambda k, shape, lo, hi: jax.random.uniform(k, shape, jnp.float32, lo, hi)
    return {
        'x_prompt': nrm(ks[0], (BATCH, SEQ, D_MODEL), 1.0),
        'x_sample': nrm(ks[1], (DEC_BATCH, DEC_SEQ, D_MODEL), 1.0),
        'cache_sb_k': nrm(ks[2], (n_pool, PAGE_SIZE, SB_HEADS, SB_HEAD_DIM), 1.0),
        'cache_sb_v': nrm(ks[3], (n_pool, PAGE_SIZE, SB_HEADS, SB_HEAD_DIM), 1.0),
        'state_rwkv_wkv': nrm(ks[4], (DEC_BATCH, RW_HEADS, RW_HEAD_DIM, RW_HEAD_DIM), 0.5),
        'state_rwkv_shift': nrm(ks[5], (DEC_BATCH, RW_COLS), 1.0),
        'page_table': jax.random.permutation(ks[6], n_pool)[:n_used].reshape(DEC_BATCH, n_pages).astype(jnp.int32),
        'c_prompt': nrm(ks[7], (BATCH, D_MODEL), 1.0),
        'c_sample': nrm(ks[8], (DEC_BATCH, D_MODEL), 1.0),
        'w_ada': nrm(ks[9], (D_MODEL, 3 * D_MODEL), 0.1 * D_MODEL ** -0.5),
        'b_ada': nrm(ks[10], (3 * D_MODEL,), 0.02),
        'norm_pre': 1.0 + nrm(ks[11], (D_MODEL,), 0.05),
        'norm_post': 1.0 + nrm(ks[12], (D_MODEL,), 0.05),
        'w_in': nrm(ks[13], (D_MODEL, IN_COLS), D_MODEL ** -0.5),
        'mu_shift': uni(ks[14], (RW_COLS,), 0.0, 1.0),
        'w0_decay': uni(ks[15], (RW_WIDTH,), -6.0, -1.0),
        'w_decay_up': nrm(ks[16], (DECAY_LORA, RW_WIDTH), 0.1 * DECAY_LORA ** -0.5),
        'a0': nrm(ks[17], (RW_WIDTH,), 0.1),
        'w_a_up': nrm(ks[18], (A_LORA, RW_WIDTH), 0.1 * A_LORA ** -0.5),
        'k_k': 0.85 + nrm(ks[19], (RW_WIDTH,), 0.05),
        'k_a': 1.0 + nrm(ks[20], (RW_WIDTH,), 0.05),
        'r_k': nrm(ks[21], (RW_HEADS, RW_HEAD_DIM), 0.1),
        'ln_x_w': 1.0 + nrm(ks[22], (RW_WIDTH,), 0.05),
        'ln_x_b': nrm(ks[23], (RW_WIDTH,), 0.02),
        'sb_bias': SB_BIAS_INIT + nrm(ks[27], (SB_HEADS,), 0.1),
        'w_branch_rwkv': nrm(ks[24], (RW_WIDTH, D_MODEL), RW_WIDTH ** -0.5),
        'w_branch_sb': nrm(ks[25], (SB_WIDTH, D_MODEL), SB_WIDTH ** -0.5),
        'w_out': nrm(ks[26], (D_MODEL, D_MODEL), D_MODEL ** -0.5),
    }


def reference(x_prompt, x_sample, cache_sb_k, cache_sb_v, state_rwkv_wkv, state_rwkv_shift,
              page_table, c_prompt, c_sample, w_ada, b_ada, norm_pre, norm_post, w_in,
              mu_shift, w0_decay, w_decay_up, a0, w_a_up, k_k, k_a, r_k, ln_x_w, ln_x_b,
              sb_bias, w_branch_rwkv, w_branch_sb, w_out):
    rw_params = (mu_shift, w0_decay, w_decay_up, a0, w_a_up, k_k, k_a, r_k, ln_x_w, ln_x_b)
    out_params = (w_branch_rwkv, w_branch_sb, w_out, norm_post)
    y_p, y_s = x_prompt, x_sample
    for _ in range(DEPTH):
        p_rw, p_sb, g_rw, g_sb, gate = project_inputs(y_p, c_prompt, w_ada, b_ada, norm_pre, w_in)
        bp = y_p.shape[0]
        zero_row = jnp.zeros((bp, RW_COLS), p_rw.dtype)
        zero_state = jnp.zeros((bp, RW_HEADS, RW_HEAD_DIM, RW_HEAD_DIM), jnp.float32)
        o_rw, wkv_p, shift_p = rwkv_branch(p_rw, zero_row, zero_state, *rw_params)
        q, k_p, v_p, z = sb_split(p_sb)
        o_sb = sb_prompt(q, k_p, v_p, sb_bias).reshape(bp, -1, SB_WIDTH) * jax.nn.silu(z)
        y_p = merge_output(y_p, o_rw, o_sb, g_rw, g_sb, gate, *out_params)
        p_rw, p_sb, g_rw, g_sb, gate = project_inputs(y_s, c_sample, w_ada, b_ada, norm_pre, w_in)
        bs = y_s.shape[0]
        o_rw, wkv_s, shift_s = rwkv_branch(p_rw, state_rwkv_shift, state_rwkv_wkv, *rw_params)
        q, k_s, v_s, z = sb_split(p_sb)
        o_sb = sb_decode(q, k_s, v_s, cache_sb_k, cache_sb_v, page_table, sb_bias).reshape(bs, -1, SB_WIDTH) * jax.nn.silu(z)
        y_s = merge_output(y_s, o_rw, o_sb, g_rw, g_sb, gate, *out_params)
    wkv_p = wkv_p.astype(state_rwkv_wkv.dtype)
    wkv_s = wkv_s.astype(state_rwkv_wkv.dtype)
    return (y_p, y_s, k_p, v_p, k_s, v_s, wkv_p, wkv_s, shift_p, shift_s)
```

```python
import functools

import numpy as np
import jax
import jax.numpy as jnp
from jax import lax
from jax.experimental import pallas as pl
from jax.experimental.pallas import tpu as pltpu

F32 = jnp.float32
BF16 = jnp.bfloat16
HIGHEST = lax.Precision.HIGHEST

RMS_EPS = 1e-6
GN_EPS = 64e-5
NORM_EPS = 1e-12

VMEM_BYTES_V7X = 64 * 1024 * 1024
VMEM_LIMIT = 48 * 1024 * 1024
SUBLANES = 8
BF16_ROWS = 16

NT_DIMS = (((1,), (1,)), ((), ()))
TN_DIMS = (((0,), (0,)), ((), ()))


def _dot(a, b, precision=None):
    return jnp.dot(a, b, preferred_element_type=F32, precision=precision)


def _dot_nt(a, b, precision=None):
    return lax.dot_general(a, b, NT_DIMS, preferred_element_type=F32, precision=precision)


def _dot_tn(a, b, precision=None):
    return lax.dot_general(a, b, TN_DIMS, preferred_element_type=F32, precision=precision)


def _split2(x):
    hi = x.astype(BF16)
    lo = (x - hi.astype(F32)).astype(BF16)
    return hi, lo


def _split3(x):
    h1 = x.astype(BF16)
    r1 = x - h1.astype(F32)
    h2 = r1.astype(BF16)
    h3 = (r1 - h2.astype(F32)).astype(BF16)
    return h1, h2, h3


def _sigmoid(x):
    return 1.0 / (1.0 + jnp.exp(-x))


def _silu(x):
    return x * _sigmoid(x)


def _softplus(x):
    return jnp.maximum(x, 0.0) + jnp.log(1.0 + jnp.exp(-jnp.abs(x)))


def _mod_kernel(c_ref, w_ref, b_ref, o_ref):
    o_ref[...] = _dot(_silu(c_ref[...]), w_ref[...], HIGHEST) + b_ref[...]


def _modulation(c_all, w_ada, b_ada):
    rows, d = c_all.shape
    n_out = w_ada.shape[1]
    tn = d
    return pl.pallas_call(
        _mod_kernel,
        grid=(n_out // tn,),
        in_specs=[pl.BlockSpec((rows, d), lambda j: (0, 0)),
                  pl.BlockSpec((d, tn), lambda j: (0, j)),
                  pl.BlockSpec((1, tn), lambda j: (0, j))],
        out_specs=pl.BlockSpec((rows, tn), lambda j: (0, j)),
        out_shape=jax.ShapeDtypeStruct((rows, n_out), F32),
        compiler_params=pltpu.CompilerParams(vmem_limit_bytes=VMEM_LIMIT),
        name="adaln_mod",
    )(c_all, w_ada, b_ada.reshape(1, n_out))


def _proj_kernel(x_ref, scale_ref, shift_ref, gain_ref, w_ref,
                 prw_ref, q_ref, k_ref, v_ref, z_ref, grw_ref, gsb_ref, *, cols, q_scale):
    x = x_ref[...]
    ms = jnp.mean(x * x, axis=-1, keepdims=True)
    h = x * lax.rsqrt(ms + RMS_EPS) * gain_ref[...]
    h = h * (1.0 + scale_ref[...]) + shift_ref[...]
    hb = h.astype(BF16)
    outs = (prw_ref, q_ref, k_ref, v_ref, z_ref, grw_ref, gsb_ref)
    for idx, ref in enumerate(outs):
        r = _dot(hb, w_ref[:, cols[idx]:cols[idx + 1]])
        if idx == 1:
            r = r * q_scale
        ref[...] = r


def _project(x, scale, shift, gain, w_bf, widths, q_scale, tm):
    rows, d = x.shape
    cols = tuple(int(c) for c in np.cumsum((0,) + tuple(widths)))
    per_row = scale.shape[0] != 1
    mod_spec = (pl.BlockSpec((tm, d), lambda i: (i, 0)) if per_row
                else pl.BlockSpec((1, d), lambda i: (0, 0)))
    return pl.pallas_call(
        functools.partial(_proj_kernel, cols=cols, q_scale=q_scale),
        grid=(rows // tm,),
        in_specs=[pl.BlockSpec((tm, d), lambda i: (i, 0)), mod_spec, mod_spec,
                  pl.BlockSpec((1, d), lambda i: (0, 0)),
                  pl.BlockSpec(w_bf.shape, lambda i: (0, 0))],
        out_specs=[pl.BlockSpec((tm, w), lambda i: (i, 0)) for w in widths],
        out_shape=[jax.ShapeDtypeStruct((rows, w), F32) for w in widths],
        compiler_params=pltpu.CompilerParams(
            dimension_semantics=("parallel",), vmem_limit_bytes=VMEM_LIMIT),
        name="in_proj",
    )(x, scale, shift, gain, w_bf)


def _rwkv_kernel(p_ref, prev0_ref, s0_ref, mu_ref, w0_ref, wdu_ref, a0_ref, wau_ref,
                 kk_ref, ka_ref, rk_ref, lnw_ref, lnb_ref, bd_ref,
                 o_ref, sout_ref, s_scr, prev_scr, y_scr, *, chunk, t_valid, n_heads, head_dim):
    c_idx = pl.program_id(1)
    n_chunks = pl.num_programs(1)
    width = n_heads * head_dim
    lora = wdu_ref.shape[0]

    @pl.when(c_idx == 0)
    def _():
        prev_scr[...] = jnp.broadcast_to(prev0_ref[0], prev_scr.shape)
        s_scr[...] = s0_ref[0]

    p = p_ref[...]
    row = lax.broadcasted_iota(jnp.int32, (chunk, 1), 0)
    prev = jnp.where(row == 0, prev_scr[0:1, :], pltpu.roll(p, 1, axis=0))
    prev_scr[...] = jnp.broadcast_to(p[chunk - 1:chunk, :], prev_scr.shape)
    m = p + (prev - p) * mu_ref[...]
    r = m[:, 0:width]
    k = m[:, width:2 * width]
    v = m[:, 2 * width:3 * width]
    z = m[:, 3 * width:4 * width]
    w_lo = m[:, 4 * width:4 * width + lora]
    a_lo = m[:, 4 * width + lora:4 * width + 2 * lora]

    bd = bd_ref[...]

    def head_sum(x):
        hi, lo = _split2(x)
        return _dot(hi, bd) + _dot(lo, bd)

    w_pre = w0_ref[...] + _dot(jnp.tanh(w_lo), wdu_ref[...], HIGHEST)
    log_w = -jnp.exp(-_softplus(-w_pre) - 0.5)
    a = _sigmoid(a0_ref[...] + _dot(a_lo, wau_ref[...], HIGHEST))
    kk = k * kk_ref[...]
    kk = kk / jnp.maximum(jnp.sqrt(head_sum(kk * kk)), NORM_EPS)
    k_mod = k * (1.0 + (a - 1.0) * ka_ref[...])
    if t_valid < chunk:
        live = row < t_valid
        log_w = jnp.where(live, log_w, 0.0)
        kk = jnp.where(live, kk, 0.0)
        k_mod = jnp.where(live, k_mod, 0.0)
        v = jnp.where(live, v, 0.0)

    ti = lax.broadcasted_iota(jnp.int32, (chunk, chunk), 0)
    si = lax.broadcasted_iota(jnp.int32, (chunk, chunk), 1)
    strict = ti > si
    lower = ti >= si
    tri = jnp.where(lower, 1.0, 0.0).astype(BF16)
    eye = jnp.where(ti == si, 1.0, 0.0).astype(F32)
    l1, l2, l3 = _split3(log_w)
    cum = _dot(tri, l1) + _dot(tri, l2) + _dot(tri, l3)
    g_t = jnp.exp(cum)
    g_inv = jnp.exp(-cum)
    at = -kk * jnp.exp(cum - log_w)
    bt = kk * a * g_inv
    kt = k_mod * g_inv
    rt = r * g_t
    g_end = g_t[chunk - 1:chunk, :]

    n_double = max(int(np.ceil(np.log2(chunk))) - 1, 0)
    for h in range(n_heads):
        sl = slice(h * head_dim, (h + 1) * head_dim)
        at_h, bt_h, kt_h, rt_h, v_h = at[:, sl], bt[:, sl], kt[:, sl], rt[:, sl], v[:, sl]
        s0 = s_scr[h]
        l_ab = jnp.where(strict, _dot_nt(at_h, bt_h, HIGHEST), 0.0)
        l_ak = jnp.where(strict, _dot_nt(at_h, kt_h, HIGHEST), 0.0)
        m_rb = jnp.where(lower, _dot_nt(rt_h, bt_h, HIGHEST), 0.0)
        m_rk = jnp.where(lower, _dot_nt(rt_h, kt_h, HIGHEST), 0.0)
        t_inv = eye + l_ab
        pw = l_ab
        for _ in range(n_double):
            pw = _dot(pw, pw, HIGHEST)
            t_inv = t_inv + _dot(t_inv, pw, HIGHEST)
        rhs = _dot_nt(at_h, s0, HIGHEST) + _dot(l_ak, v_h, HIGHEST)
        u = _dot(t_inv, rhs, HIGHEST)
        y = _dot_nt(rt_h, s0, HIGHEST) + _dot(m_rb, u, HIGHEST) + _dot(m_rk, v_h, HIGHEST)
        s_new = (s0 + _dot_tn(u, bt_h, HIGHEST) + _dot_tn(v_h, kt_h, HIGHEST)) * g_end[:, sl]
        s_scr[h] = s_new
        y_scr[:, sl] = y

    y = y_scr[...]
    inv_n = 1.0 / head_dim
    mean = head_sum(y) * inv_n
    d = y - mean
    var = head_sum(d * d) * inv_n
    yn = d * lax.rsqrt(var + GN_EPS) * lnw_ref[...] + lnb_ref[...]
    bonus = head_sum(r * k_mod * rk_ref[...])
    o_ref[...] = (yn + bonus * v) * _silu(z)

    @pl.when(c_idx == n_chunks - 1)
    def _():
        sout_ref[0] = s_scr[...]


def _rwkv(p_rw, prev0, s0, params, bd, *, n_seq, chunk, t_valid):
    rows, cols = p_rw.shape
    n_chunks = rows // (n_seq * chunk)
    n_heads, head_dim = s0.shape[1], s0.shape[2]
    width = n_heads * head_dim
    const = lambda arr: pl.BlockSpec(arr.shape, lambda s, c: (0,) * arr.ndim)
    return pl.pallas_call(
        functools.partial(_rwkv_kernel, chunk=chunk, t_valid=t_valid,
                          n_heads=n_heads, head_dim=head_dim),
        grid=(n_seq, n_chunks),
        in_specs=[pl.BlockSpec((chunk, cols), lambda s, c: (s * n_chunks + c, 0)),
                  pl.BlockSpec((1, 1, cols), lambda s, c: (s, 0, 0)),
                  pl.BlockSpec((1, n_heads, head_dim, head_dim), lambda s, c: (s, 0, 0, 0))]
                 + [const(a) for a in params] + [const(bd)],
        out_specs=[pl.BlockSpec((chunk, width), lambda s, c: (s * n_chunks + c, 0)),
                   pl.BlockSpec((1, n_heads, head_dim, head_dim), lambda s, c: (s, 0, 0, 0))],
        out_shape=[jax.ShapeDtypeStruct((rows, width), F32),
                   jax.ShapeDtypeStruct(s0.shape, F32)],
        scratch_shapes=[pltpu.VMEM((n_heads, head_dim, head_dim), F32),
                        pltpu.VMEM((SUBLANES, cols), F32),
                        pltpu.VMEM((chunk, width), F32)],
        compiler_params=pltpu.CompilerParams(
            dimension_semantics=("parallel", "arbitrary"), vmem_limit_bytes=VMEM_LIMIT),
        name="rwkv7_chunk",
    )(p_rw, prev0, s0, *params, bd)


def _sb_prompt_kernel(qi_ref, kj_ref, bias_ref, q_ref, k_ref, v_ref, m_ref, o_ref,
                      carry_scr, acc_scr, *, blk, n_heads, head_dim):
    step = pl.program_id(0)
    qi = qi_ref[step]
    kj = kj_ref[step]
    lanes = carry_scr.shape[-1]

    def sweep(diagonal):
        qb = q_ref[...].astype(BF16)
        kb = k_ref[...].astype(BF16)
        vb = v_ref[...].astype(BF16)
        if diagonal:
            ti = lax.broadcasted_iota(jnp.int32, (blk, blk), 0)
            si = lax.broadcasted_iota(jnp.int32, (blk, blk), 1)
            causal = si < ti
        for h in range(n_heads):
            sl = slice(h * head_dim, (h + 1) * head_dim)
            zz = _dot_nt(qb[:, sl], kb[:, sl]) + bias_ref[h]
            sp = _softplus(zz)
            log_fail = -sp
            if diagonal:
                log_fail = jnp.where(causal, log_fail, 0.0)
                carry = jnp.zeros((blk, lanes), F32)
            else:
                carry = carry_scr[h]
            rest = _dot(log_fail.astype(BF16), m_ref[...]) + pltpu.repeat(carry, blk // lanes, axis=1)
            wgt = jnp.exp(zz - sp + rest)
            if diagonal:
                wgt = jnp.where(causal, wgt, 0.0)
            contrib = _dot(wgt.astype(BF16), vb[:, sl])
            total = jnp.sum(log_fail, axis=-1, keepdims=True)
            carry_scr[h] = carry + total
            if diagonal:
                acc_scr[h] = contrib
            else:
                acc_scr[h] = acc_scr[h] + contrib

    @pl.when(kj == qi)
    def _():
        sweep(True)

    @pl.when(kj != qi)
    def _():
        sweep(False)

    @pl.when(kj == 0)
    def _():
        for h in range(n_heads):
            o_ref[:, h * head_dim:(h + 1) * head_dim] = acc_scr[h]


def _sb_prompt(q, k, v, bias, blk, n_heads):
    rows, width = q.shape
    head_dim = width // n_heads
    nq = rows // blk
    pairs = [(i, j) for i in range(nq) for j in range(i, -1, -1)]
    qi = jnp.asarray([p[0] for p in pairs], jnp.int32)
    kj = jnp.asarray([p[1] for p in pairs], jnp.int32)
    excl = jnp.asarray(np.tril(np.ones((blk, blk), np.float32), -1), BF16)
    grid_spec = pltpu.PrefetchScalarGridSpec(
        num_scalar_prefetch=3,
        grid=(len(pairs),),
        in_specs=[pl.BlockSpec((blk, width), lambda s, qi, kj, b: (qi[s], 0)),
                  pl.BlockSpec((blk, width), lambda s, qi, kj, b: (kj[s], 0)),
                  pl.BlockSpec((blk, width), lambda s, qi, kj, b: (kj[s], 0)),
                  pl.BlockSpec((blk, blk), lambda s, qi, kj, b: (0, 0))],
        out_specs=pl.BlockSpec((blk, width), lambda s, qi, kj, b: (qi[s], 0)),
        scratch_shapes=[pltpu.VMEM((n_heads, blk, 128), F32),
                        pltpu.VMEM((n_heads, blk, head_dim), F32)],
    )
    return pl.pallas_call(
        functools.partial(_sb_prompt_kernel, blk=blk, n_heads=n_heads, head_dim=head_dim),
        grid_spec=grid_spec,
        out_shape=jax.ShapeDtypeStruct((rows, width), F32),
        compiler_params=pltpu.CompilerParams(
            dimension_semantics=("arbitrary",), vmem_limit_bytes=VMEM_LIMIT),
        name="sb_prompt",
    )(qi, kj, bias, q, k, v, excl)


def _sb_decode_kernel(pt_ref, bias_ref, q_ref, kn_ref, vn_ref, m_ref, m_new_ref, *rest,
                      pages_per_step, n_heads, head_dim, n_new):
    k_refs = rest[:pages_per_step]
    v_refs = rest[pages_per_step:2 * pages_per_step]
    o_ref = rest[2 * pages_per_step]
    carry_scr, acc_scr = rest[2 * pages_per_step + 1:]
    step = pl.program_id(1)
    n_rows = n_new * n_heads
    width = n_heads * head_dim
    page = m_ref.shape[0]

    row_head = lax.broadcasted_iota(jnp.int32, (n_heads, width), 0)
    lane_head = lax.broadcasted_iota(jnp.int32, (n_heads, width), 1) // head_dim
    head_mask = row_head == lane_head
    q = q_ref[0]
    q_bd = jnp.concatenate(
        [jnp.where(head_mask, jnp.broadcast_to(q[t:t + 1, :], (n_heads, width)), 0.0)
         for t in range(n_new)], axis=0).astype(BF16)
    rh = lax.broadcasted_iota(jnp.int32, (n_rows, 1), 0) % n_heads
    bias_col = jnp.zeros((n_rows, 1), F32)
    for h in range(n_heads):
        bias_col = jnp.where(rh == h, bias_ref[h], bias_col)

    @pl.when(step == 0)
    def _():
        pad = kn_ref.shape[1]
        kb = kn_ref[0].astype(BF16)
        vb = vn_ref[0].astype(BF16)
        zz = _dot_nt(q_bd, kb) + bias_col
        tq = lax.broadcasted_iota(jnp.int32, (n_rows, pad), 0) // n_heads
        tk = lax.broadcasted_iota(jnp.int32, (n_rows, pad), 1)
        causal = tk < tq
        sp = _softplus(zz)
        log_fail = jnp.where(causal, -sp, 0.0)
        rest_new = _dot(log_fail, m_new_ref[...], HIGHEST)
        wgt = jnp.where(causal, jnp.exp(zz - sp + rest_new), 0.0)
        acc_scr[...] = _dot(wgt.astype(BF16), vb)
        carry_scr[...] = jnp.broadcast_to(jnp.sum(log_fail, axis=-1, keepdims=True), carry_scr.shape)

    acc = acc_scr[...]
    carry = carry_scr[...]
    for i in range(pages_per_step):
        kb = k_refs[i][0].astype(BF16)
        vb = v_refs[i][0].astype(BF16)
        zz = _dot_nt(q_bd, kb) + bias_col
        sp = _softplus(zz)
        log_fail = -sp
        rest_p = _dot(log_fail.astype(BF16), m_ref[...]) + carry
        wgt = jnp.exp(zz - sp + rest_p)
        acc = acc + _dot(wgt.astype(BF16), vb)
        carry = carry + jnp.sum(log_fail, axis=-1, keepdims=True)
    acc_scr[...] = acc
    carry_scr[...] = carry

    @pl.when(step == pl.num_programs(1) - 1)
    def _():
        for t in range(n_new):
            blk = jnp.where(head_mask, acc[t * n_heads:(t + 1) * n_heads, :], 0.0)
            o_ref[0, t:t + 1, :] = jnp.sum(blk, axis=0, keepdims=True)


def _sb_decode(q, k_new, v_new, cache_k, cache_v, page_table, bias, pages_per_step, n_heads):
    n_seq, n_new, width = q.shape
    head_dim = width // n_heads
    n_pages = page_table.shape[1]
    page = cache_k.shape[1]
    n_steps = n_pages // pages_per_step
    pad = k_new.shape[1]
    n_rows = n_new * n_heads
    excl = jnp.asarray(np.tril(np.ones((page, page), np.float32), -1), BF16)
    excl_new = jnp.asarray(np.tril(np.ones((pad, pad), np.float32), -1), F32)

    def page_spec(i):
        def index_map(b, s, pt, bias_):
            logical = n_pages - 1 - (s * pages_per_step + i)
            return (pt[b * n_pages + logical], 0, 0)
        return pl.BlockSpec((1, page, width), index_map)

    seq_spec = lambda rows: pl.BlockSpec((1, rows, width), lambda b, s, pt, bias_: (b, 0, 0))
    grid_spec = pltpu.PrefetchScalarGridSpec(
        num_scalar_prefetch=2,
        grid=(n_seq, n_steps),
        in_specs=[seq_spec(n_new), seq_spec(pad), seq_spec(pad),
                  pl.BlockSpec((page, page), lambda b, s, pt, bias_: (0, 0)),
                  pl.BlockSpec((pad, pad), lambda b, s, pt, bias_: (0, 0))]
                 + [page_spec(i) for i in range(pages_per_step)] * 2,
        out_specs=seq_spec(n_new),
        scratch_shapes=[pltpu.VMEM((n_rows, page), F32), pltpu.VMEM((n_rows, width), F32)],
    )
    return pl.pallas_call(
        functools.partial(_sb_decode_kernel, pages_per_step=pages_per_step, n_heads=n_heads,
                          head_dim=head_dim, n_new=n_new),
        grid_spec=grid_spec,
        out_shape=jax.ShapeDtypeStruct((n_seq, n_new, width), F32),
        compiler_params=pltpu.CompilerParams(
            dimension_semantics=("parallel", "arbitrary"), vmem_limit_bytes=VMEM_LIMIT),
        name="sb_decode",
    )(page_table.reshape(-1), bias, q, k_new, v_new, excl, excl_new,
      *([cache_k] * pages_per_step), *([cache_v] * pages_per_step))


def _merge_kernel(x_ref, orw_ref, osb_ref, zsb_ref, grw_ref, gsb_ref, gate_ref, gain_ref,
                  wrw_ref, wsb_ref, wout_ref, y_ref):
    o_sb = osb_ref[...] * _silu(zsb_ref[...])
    m = (_sigmoid(grw_ref[...]) * _dot(orw_ref[...].astype(BF16), wrw_ref[...])
         + _sigmoid(gsb_ref[...]) * _dot(o_sb.astype(BF16), wsb_ref[...]))
    u = _dot(m.astype(BF16), wout_ref[...])
    ms = jnp.mean(u * u, axis=-1, keepdims=True)
    y_ref[...] = x_ref[...] + gate_ref[...] * (u * lax.rsqrt(ms + RMS_EPS) * gain_ref[...])


def _merge(x, o_rw, o_sb, z_sb, g_rw, g_sb, gate, gain, w_rw, w_sb, w_out, tm):
    rows, d = x.shape
    width = o_rw.shape[1]
    per_row = gate.shape[0] != 1
    gate_spec = (pl.BlockSpec((tm, d), lambda i: (i, 0)) if per_row
                 else pl.BlockSpec((1, d), lambda i: (0, 0)))
    row_spec = lambda w: pl.BlockSpec((tm, w), lambda i: (i, 0))
    const = lambda arr: pl.BlockSpec(arr.shape, lambda i: (0, 0))
    return pl.pallas_call(
        _merge_kernel,
        grid=(rows // tm,),
        in_specs=[row_spec(d), row_spec(width), row_spec(width), row_spec(width),
                  row_spec(d), row_spec(d), gate_spec, const(gain),
                  const(w_rw), const(w_sb), const(w_out)],
        out_specs=row_spec(d),
        out_shape=jax.ShapeDtypeStruct((rows, d), F32),
        compiler_params=pltpu.CompilerParams(
            dimension_semantics=("parallel",), vmem_limit_bytes=VMEM_LIMIT),
        name="merge_out",
    )(x, o_rw, o_sb, z_sb, g_rw, g_sb, gate, gain, w_rw, w_sb, w_out)


def _row_tile(rows, want):
    t = min(rows, want)
    while rows % t:
        t //= 2
    return t


def kernel(x_prompt, x_sample, cache_sb_k, cache_sb_v, state_rwkv_wkv, state_rwkv_shift, page_table, c_prompt, c_sample, w_ada, b_ada, norm_pre, norm_post, w_in, mu_shift, w0_decay, w_decay_up, a0, w_a_up, k_k, k_a, r_k, ln_x_w, ln_x_b, sb_bias, w_branch_rwkv, w_branch_sb, w_out):
    bp, seq, d = x_prompt.shape
    bs, t_new, _ = x_sample.shape
    rw_heads, rw_dim = r_k.shape
    rw_width = rw_heads * rw_dim
    rw_cols = mu_shift.shape[0]
    n_pool, page, sb_heads, sb_dim = cache_sb_k.shape
    sb_width = sb_heads * sb_dim
    widths = (rw_cols, sb_width, sb_width, sb_width, sb_width, d, d)
    assert sum(widths) == w_in.shape[1]
    q_scale = float(sb_dim) ** -0.5

    n_c = bp + bs
    c_all = jnp.concatenate([c_prompt, c_sample], axis=0)
    c_all = jnp.pad(c_all, ((0, (-n_c) % SUBLANES), (0, 0)))
    mod = _modulation(c_all, w_ada, b_ada)
    shift_all, scale_all, gate_all = mod[:, :d], mod[:, d:2 * d], mod[:, 2 * d:]

    w_in_bf = w_in.astype(BF16)
    w_rw_bf = w_branch_rwkv.astype(BF16)
    w_sb_bf = w_branch_sb.astype(BF16)
    w_out_bf = w_out.astype(BF16)
    gain_pre = norm_pre.reshape(1, d)
    gain_post = norm_post.reshape(1, d)
    row2 = lambda a: a.reshape(1, -1)
    rw_params = (row2(mu_shift), row2(w0_decay), w_decay_up, row2(a0), w_a_up, row2(k_k),
                 row2(k_a), row2(r_k), row2(ln_x_w), row2(ln_x_b))
    head_of = np.arange(rw_width) // rw_dim
    bd = jnp.asarray(head_of[:, None] == head_of[None, :], BF16)

    outs_p = []
    for b in range(bp):
        x = x_prompt[b]
        tm = _row_tile(seq, 256)
        p_rw, q, k, v, z, g_rw, g_sb = _project(
            x, scale_all[b:b + 1], shift_all[b:b + 1], gain_pre, w_in_bf, widths, q_scale, tm)
        chunk = _row_tile(seq, 64)
        o_rw, wkv = _rwkv(p_rw, jnp.zeros((1, 1, rw_cols), F32),
                          jnp.zeros((1, rw_heads, rw_dim, rw_dim), F32), rw_params, bd,
                          n_seq=1, chunk=chunk, t_valid=chunk)
        o_sb = _sb_prompt(q, k, v, sb_bias, _row_tile(seq, 256), sb_heads)
        y = _merge(x, o_rw, o_sb, z, g_rw, g_sb, gate_all[b:b + 1], gain_post,
                   w_rw_bf, w_sb_bf, w_out_bf, _row_tile(seq, 512))
        outs_p.append((y, k, v, wkv[0], p_rw[seq - 1]))
    y_p = jnp.stack([o[0] for o in outs_p])
    k_p = jnp.stack([o[1] for o in outs_p]).reshape(bp, seq, sb_heads, sb_dim)
    v_p = jnp.stack([o[2] for o in outs_p]).reshape(bp, seq, sb_heads, sb_dim)
    wkv_p = jnp.stack([o[3] for o in outs_p])
    shift_p = jnp.stack([o[4] for o in outs_p])

    rows_s = bs * t_new
    xs = x_sample.reshape(rows_s, d)
    expand = lambda a: jnp.repeat(a[bp:bp + bs], t_new, axis=0)
    tm = _row_tile(rows_s, 256)
    p_rw, q, k, v, z, g_rw, g_sb = _project(
        xs, expand(scale_all), expand(shift_all), gain_pre, w_in_bf, widths, q_scale, tm)
    t_pad = -(-t_new // BF16_ROWS) * BF16_ROWS
    pad_rows = lambda a: jnp.pad(a.reshape(bs, t_new, -1), ((0, 0), (0, t_pad - t_new), (0, 0)))
    o_rw, wkv_s = _rwkv(pad_rows(p_rw).reshape(bs * t_pad, rw_cols),
                        state_rwkv_shift.reshape(bs, 1, rw_cols), state_rwkv_wkv, rw_params, bd,
                        n_seq=bs, chunk=t_pad, t_valid=t_new)
    o_rw = o_rw.reshape(bs, t_pad, rw_width)[:, :t_new].reshape(rows_s, rw_width)
    n_pages = page_table.shape[1]
    pages_per_step = 8 if n_pages % 8 == 0 else 1
    o_sb = _sb_decode(q.reshape(bs, t_new, sb_width), pad_rows(k), pad_rows(v),
                      cache_sb_k.reshape(n_pool, page, sb_width),
                      cache_sb_v.reshape(n_pool, page, sb_width),
                      page_table, sb_bias, pages_per_step, sb_heads)
    y_s = _merge(xs, o_rw, o_sb.reshape(rows_s, sb_width), z, g_rw, g_sb, expand(gate_all),
                 gain_post, w_rw_bf, w_sb_bf, w_out_bf, _row_tile(rows_s, 512))
    y_s = y_s.reshape(bs, t_new, d)
    k_s = k.reshape(bs, t_new, sb_heads, sb_dim)
    v_s = v.reshape(bs, t_new, sb_heads, sb_dim)
    shift_s = p_rw.reshape(bs, t_new, rw_cols)[:, -1]
    return (y_p, y_s, k_p, v_p, k_s, v_s, wkv_p, wkv_s, shift_p, shift_s)
```

```python
import functools

import numpy as np
import jax
import jax.numpy as jnp
from jax import lax
from jax.experimental import pallas as pl
from jax.experimental.pallas import tpu as pltpu

F32 = jnp.float32
BF16 = jnp.bfloat16
HIGHEST = lax.Precision.HIGHEST

RMS_EPS = 1e-6
GN_EPS = 64e-5
NORM_EPS = 1e-12

VMEM_BYTES_V7X = 64 * 1024 * 1024
VMEM_LIMIT = 48 * 1024 * 1024
SUBLANES = 8
LANES = 128
BF16_ROWS = 16
LOG2_E = 1.4426950408889634

NT_DIMS = (((1,), (1,)), ((), ()))
TN_DIMS = (((0,), (0,)), ((), ()))


def _dot(a, b, precision=None):
    return jnp.dot(a, b, preferred_element_type=F32, precision=precision)


def _dot_nt(a, b, precision=None):
    return lax.dot_general(a, b, NT_DIMS, preferred_element_type=F32, precision=precision)


def _dot_tn(a, b, precision=None):
    return lax.dot_general(a, b, TN_DIMS, preferred_element_type=F32, precision=precision)


def _split2(x):
    hi = x.astype(BF16)
    lo = (x - hi.astype(F32)).astype(BF16)
    return hi, lo


def _split3(x):
    h1 = x.astype(BF16)
    r1 = x - h1.astype(F32)
    h2 = r1.astype(BF16)
    h3 = (r1 - h2.astype(F32)).astype(BF16)
    return h1, h2, h3


def _sigmoid(x):
    return 1.0 / (1.0 + jnp.exp(-x))


def _silu(x):
    return x * _sigmoid(x)


def _softplus(x):
    return jnp.maximum(x, 0.0) + jnp.log(1.0 + jnp.exp(-jnp.abs(x)))


def _mod_kernel(c_ref, w_ref, b_ref, o_ref):
    o_ref[...] = _dot(_silu(c_ref[...]), w_ref[...], HIGHEST) + b_ref[...]


def _modulation(c_all, w_ada, b_ada):
    rows, d = c_all.shape
    n_out = w_ada.shape[1]
    tn = d
    return pl.pallas_call(
        _mod_kernel,
        grid=(n_out // tn,),
        in_specs=[pl.BlockSpec((rows, d), lambda j: (0, 0)),
                  pl.BlockSpec((d, tn), lambda j: (0, j)),
                  pl.BlockSpec((1, tn), lambda j: (0, j))],
        out_specs=pl.BlockSpec((rows, tn), lambda j: (0, j)),
        out_shape=jax.ShapeDtypeStruct((rows, n_out), F32),
        compiler_params=pltpu.CompilerParams(vmem_limit_bytes=VMEM_LIMIT),
        name="adaln_mod",
    )(c_all, w_ada, b_ada.reshape(1, n_out))


def _proj_kernel(x_ref, scale_ref, shift_ref, gain_ref, w_ref,
                 prw_ref, q_ref, k_ref, v_ref, z_ref, grw_ref, gsb_ref, *, cols, q_scale):
    x = x_ref[...]
    ms = jnp.mean(x * x, axis=-1, keepdims=True)
    h = x * lax.rsqrt(ms + RMS_EPS) * gain_ref[...]
    h = h * (1.0 + scale_ref[...]) + shift_ref[...]
    hb = h.astype(BF16)
    outs = (prw_ref, q_ref, k_ref, v_ref, z_ref, grw_ref, gsb_ref)
    for idx, ref in enumerate(outs):
        r = _dot(hb, w_ref[:, cols[idx]:cols[idx + 1]])
        if idx == 1:
            r = r * q_scale
        ref[...] = r


def _project(x, scale, shift, gain, w_bf, widths, q_scale, tm):
    rows, d = x.shape
    cols = tuple(int(c) for c in np.cumsum((0,) + tuple(widths)))
    per_row = scale.shape[0] != 1
    mod_spec = (pl.BlockSpec((tm, d), lambda i: (i, 0)) if per_row
                else pl.BlockSpec((1, d), lambda i: (0, 0)))
    return pl.pallas_call(
        functools.partial(_proj_kernel, cols=cols, q_scale=q_scale),
        grid=(rows // tm,),
        in_specs=[pl.BlockSpec((tm, d), lambda i: (i, 0)), mod_spec, mod_spec,
                  pl.BlockSpec((1, d), lambda i: (0, 0)),
                  pl.BlockSpec(w_bf.shape, lambda i: (0, 0))],
        out_specs=[pl.BlockSpec((tm, w), lambda i: (i, 0)) for w in widths],
        out_shape=[jax.ShapeDtypeStruct((rows, w), F32) for w in widths],
        compiler_params=pltpu.CompilerParams(
            dimension_semantics=("parallel",), vmem_limit_bytes=VMEM_LIMIT),
        name="in_proj",
    )(x, scale, shift, gain, w_bf)


def _rwkv_kernel(p_ref, prev0_ref, s0_ref, mu_ref, w0_ref, wdu_ref, a0_ref, wau_ref,
                 kk_ref, ka_ref, rk_ref, lnw_ref, lnb_ref, bd_ref,
                 o_ref, sout_ref, s_scr, prev_scr, y_scr, *, chunk, t_valid, n_heads, head_dim, group):
    c_idx = pl.program_id(1)
    n_chunks = pl.num_programs(1)

    @pl.when(c_idx == 0)
    def _():
        for g in range(group):
            prev_scr[g] = jnp.broadcast_to(prev0_ref[g], prev_scr.shape[1:])
        s_scr[...] = s0_ref[...]

    consts = (mu_ref, w0_ref, wdu_ref, a0_ref, wau_ref, kk_ref, ka_ref, rk_ref, lnw_ref, lnb_ref, bd_ref)
    o_ref[...] = _rwkv_chunk(p_ref[...], consts, s_scr, prev_scr, y_scr, chunk=chunk, t_valid=t_valid,
                             n_heads=n_heads, head_dim=head_dim, group=group)

    @pl.when(c_idx == n_chunks - 1)
    def _():
        sout_ref[...] = s_scr[...]


def _rwkv_chunk(p, consts, s_scr, prev_scr, y_scr, *, chunk, t_valid, n_heads, head_dim, group):
    mu_ref, w0_ref, wdu_ref, a0_ref, wau_ref, kk_ref, ka_ref, rk_ref, lnw_ref, lnb_ref, bd_ref = consts
    width = n_heads * head_dim
    lora = wdu_ref.shape[0]
    n_rows = group * chunk
    row_all = lax.broadcasted_iota(jnp.int32, (n_rows, 1), 0)
    row = row_all % chunk
    prev = pltpu.roll(p, 1, axis=0)
    for g in range(group):
        prev = jnp.where(row_all == g * chunk, prev_scr[g, 0:1, :], prev)
        prev_scr[g] = jnp.broadcast_to(p[(g + 1) * chunk - 1:(g + 1) * chunk, :], prev_scr.shape[1:])
    m = p + (prev - p) * mu_ref[...]
    r = m[:, 0:width]
    k = m[:, width:2 * width]
    v = m[:, 2 * width:3 * width]
    z = m[:, 3 * width:4 * width]
    w_lo = m[:, 4 * width:4 * width + lora]
    a_lo = m[:, 4 * width + lora:4 * width + 2 * lora]

    bd = bd_ref[...]

    def head_sum(x):
        hi, lo = _split2(x)
        return _dot(hi, bd) + _dot(lo, bd)

    w_pre = w0_ref[...] + _dot(jnp.tanh(w_lo), wdu_ref[...], HIGHEST)
    log_w = -jnp.exp(-_softplus(-w_pre) - 0.5)
    a = _sigmoid(a0_ref[...] + _dot(a_lo, wau_ref[...], HIGHEST))
    kk = k * kk_ref[...]
    kk = kk / jnp.maximum(jnp.sqrt(head_sum(kk * kk)), NORM_EPS)
    k_mod = k * (1.0 + (a - 1.0) * ka_ref[...])
    if t_valid < chunk:
        live = row < t_valid
        log_w = jnp.where(live, log_w, 0.0)
        kk = jnp.where(live, kk, 0.0)
        k_mod = jnp.where(live, k_mod, 0.0)
        v = jnp.where(live, v, 0.0)

    ti = lax.broadcasted_iota(jnp.int32, (chunk, chunk), 0)
    si = lax.broadcasted_iota(jnp.int32, (chunk, chunk), 1)
    strict = ti > si
    lower = ti >= si
    eye = jnp.where(ti == si, 1.0, 0.0).astype(F32)
    ta = lax.broadcasted_iota(jnp.int32, (n_rows, n_rows), 0)
    sa = lax.broadcasted_iota(jnp.int32, (n_rows, n_rows), 1)
    tri = jnp.where((ta >= sa) & (ta // chunk == sa // chunk), 1.0, 0.0).astype(BF16)
    l1, l2, l3 = _split3(log_w)
    cum = _dot(tri, l1) + _dot(tri, l2) + _dot(tri, l3)
    g_t = jnp.exp(cum)
    g_inv = jnp.exp(-cum)
    at = -kk * jnp.exp(cum - log_w)
    bt = kk * a * g_inv
    kt = k_mod * g_inv
    rt = r * g_t

    n_double = max(int(np.ceil(np.log2(min(chunk, t_valid)))) - 1, 0)
    at_b, bt_b, kt_b, rt_b, v_b = (x.astype(BF16) for x in (at, bt, kt, rt, v))
    bt_lo = (bt - bt_b.astype(F32)).astype(BF16)
    kt_lo = (kt - kt_b.astype(F32)).astype(BF16)
    v_lo = (v - v_b.astype(F32)).astype(BF16)
    inst = [(g, h) for g in range(group) for h in range(n_heads)]
    ids = range(len(inst))
    cut = lambda x, g, h: x[g * chunk:(g + 1) * chunk, h * head_dim:(h + 1) * head_dim]
    s0 = [s_scr[g, h] for g, h in inst]
    ar = [jnp.concatenate([cut(at_b, g, h), cut(rt_b, g, h)], axis=0) for g, h in inst]
    g_b = [_dot_nt(ar[i], cut(bt_b, *inst[i])) for i in ids]
    g_k = [_dot_nt(ar[i], cut(kt_b, *inst[i])) for i in ids]
    from_state = [_dot_nt(ar[i], s0[i].astype(BF16)) for i in ids]
    l_ab = [jnp.where(strict, x[:chunk], 0.0) for x in g_b]
    l_ak = [jnp.where(strict, x[:chunk], 0.0).astype(BF16) for x in g_k]
    m_rb = [jnp.where(lower, x[chunk:], 0.0).astype(BF16) for x in g_b]
    m_rk = [jnp.where(lower, x[chunk:], 0.0).astype(BF16) for x in g_k]
    rhs = [from_state[i][:chunk] + _dot(l_ak[i], cut(v_b, *inst[i])) for i in ids]
    y_kv = [from_state[i][chunk:] + _dot(m_rk[i], cut(v_b, *inst[i])) for i in ids]
    t_inv = [eye + l for l in l_ab]
    pw_b = [l.astype(BF16) for l in l_ab]
    if n_double:
        pw_b = [_dot(x, x).astype(BF16) for x in pw_b]
    for j in range(n_double):
        t_b = [t.astype(BF16) for t in t_inv]
        t_inv = [t_inv[i] + _dot(t_b[i], pw_b[i]) for i in ids]
        if j + 1 < n_double:
            pw_b = [_dot(x, x).astype(BF16) for x in pw_b]
    u = [_dot(t_inv[i].astype(BF16), rhs[i].astype(BF16)) for i in ids]
    u_b = [x.astype(BF16) for x in u]
    for i in ids:
        g, h = inst[i]
        y_scr[g * chunk:(g + 1) * chunk, h * head_dim:(h + 1) * head_dim] = y_kv[i] + _dot(m_rb[i], u_b[i])
        uv_hi = jnp.concatenate([u_b[i], cut(v_b, g, h)], axis=0)
        uv_lo = jnp.concatenate([(u[i] - u_b[i].astype(F32)).astype(BF16), cut(v_lo, g, h)], axis=0)
        bk_hi = jnp.concatenate([cut(bt_b, g, h), cut(kt_b, g, h)], axis=0)
        bk_lo = jnp.concatenate([cut(bt_lo, g, h), cut(kt_lo, g, h)], axis=0)
        upd = _dot_tn(uv_hi, bk_hi) + _dot_tn(uv_hi, bk_lo) + _dot_tn(uv_lo, bk_hi)
        g_end = g_t[(g + 1) * chunk - 1:(g + 1) * chunk, h * head_dim:(h + 1) * head_dim]
        s_scr[g, h] = (s0[i] + upd) * g_end

    y = y_scr[...]
    inv_n = 1.0 / head_dim
    mean = head_sum(y) * inv_n
    d = y - mean
    var = head_sum(d * d) * inv_n
    yn = d * lax.rsqrt(var + GN_EPS) * lnw_ref[...] + lnb_ref[...]
    bonus = head_sum(r * k_mod * rk_ref[...])
    return (yn + bonus * v) * _silu(z)


def _rwkv(p_rw, prev0, s0, params, bd, *, n_seq, chunk, t_valid, group):
    rows, cols = p_rw.shape
    n_chunks = rows // (n_seq * chunk)
    assert group == 1 or n_chunks == 1
    n_heads, head_dim = s0.shape[1], s0.shape[2]
    width = n_heads * head_dim
    const = lambda arr: pl.BlockSpec(arr.shape, lambda s, c: (0,) * arr.ndim)
    state_spec = pl.BlockSpec((group, n_heads, head_dim, head_dim), lambda s, c: (s, 0, 0, 0))
    return pl.pallas_call(
        functools.partial(_rwkv_kernel, chunk=chunk, t_valid=t_valid,
                          n_heads=n_heads, head_dim=head_dim, group=group),
        grid=(n_seq // group, n_chunks),
        in_specs=[pl.BlockSpec((group * chunk, cols), lambda s, c: (s * n_chunks + c, 0)),
                  pl.BlockSpec((group, 1, cols), lambda s, c: (s, 0, 0)),
                  state_spec]
                 + [const(a) for a in params] + [const(bd)],
        out_specs=[pl.BlockSpec((group * chunk, width), lambda s, c: (s * n_chunks + c, 0)),
                   state_spec],
        out_shape=[jax.ShapeDtypeStruct((rows, width), F32),
                   jax.ShapeDtypeStruct(s0.shape, F32)],
        scratch_shapes=[pltpu.VMEM((group, n_heads, head_dim, head_dim), F32),
                        pltpu.VMEM((group, SUBLANES, cols), F32),
                        pltpu.VMEM((group * chunk, width), F32)],
        compiler_params=pltpu.CompilerParams(
            dimension_semantics=("parallel", "arbitrary"), vmem_limit_bytes=VMEM_LIMIT),
        name="rwkv7_chunk",
    )(p_rw, prev0, s0, *params, bd)


def _softplus2(x):
    return jnp.maximum(x, 0.0) + jnp.log2(1.0 + jnp.exp2(-jnp.abs(x)))


def _suffix_matrix(n, with_total):
    m = -np.tril(np.ones((n, n), np.float32), -1)
    return np.concatenate([m, -np.ones((n, LANES), np.float32)], axis=1) if with_total else m


def _sb_prompt_kernel(qi_ref, kj_ref, bias_ref, q_ref, k_ref, v_ref, m_ref, o_ref,
                      carry_scr, acc_scr, *, blk, n_heads, head_dim):
    step = pl.program_id(0)
    qi = qi_ref[step]
    kj = kj_ref[step]
    lanes = carry_scr.shape[-1]

    def sweep(diagonal):
        qb = q_ref[...].astype(BF16)
        kb = k_ref[...].astype(BF16)
        vb = v_ref[...].astype(BF16)
        if diagonal:
            ti = lax.broadcasted_iota(jnp.int32, (blk, blk), 0)
            si = lax.broadcasted_iota(jnp.int32, (blk, blk), 1)
            causal = si < ti
        sl = lambda h: slice(h * head_dim, (h + 1) * head_dim)
        zz, log_beta, sums, first = {}, {}, {}, {}
        for i in range(n_heads + 2):
            if i < n_heads:
                zz[i] = _dot_nt(qb[:, sl(i)], kb[:, sl(i)]) + bias_ref[i]
            h = i - 1
            if 0 <= h < n_heads:
                z_h = zz.pop(h)
                sp = _softplus2(z_h)
                log_beta[h] = z_h - sp
                fail = jnp.where(causal, sp, 0.0) if diagonal else sp
                first[h] = fail[:, 0:1]
                sums[h] = _dot(fail.astype(BF16), m_ref[...])
            h = i - 2
            if 0 <= h < n_heads:
                rest = sums.pop(h)
                total = jnp.broadcast_to(rest[:, 0:1] - first.pop(h), (blk, lanes))
                if not diagonal:
                    rest = rest + pltpu.repeat(carry_scr[h], blk // lanes, axis=1)
                wgt = jnp.exp2(log_beta.pop(h) + rest)
                if diagonal:
                    wgt = jnp.where(causal, wgt, 0.0)
                contrib = _dot(wgt.astype(BF16), vb[:, sl(h)])
                if diagonal:
                    carry_scr[h] = total
                    acc_scr[h] = contrib
                else:
                    carry_scr[h] = carry_scr[h] + total
                    acc_scr[h] = acc_scr[h] + contrib

    @pl.when(kj == qi)
    def _():
        sweep(True)

    @pl.when(kj != qi)
    def _():
        sweep(False)

    @pl.when(kj == 0)
    def _():
        for h in range(n_heads):
            o_ref[:, h * head_dim:(h + 1) * head_dim] = acc_scr[h]


def _sb_prompt(q, k, v, bias, blk, n_heads):
    rows, width = q.shape
    head_dim = width // n_heads
    nq = rows // blk
    pairs = [(i, j) for i in range(nq) for j in range(i, -1, -1)]
    qi = jnp.asarray([p[0] for p in pairs], jnp.int32)
    kj = jnp.asarray([p[1] for p in pairs], jnp.int32)
    excl = jnp.asarray(_suffix_matrix(blk, False), BF16)
    grid_spec = pltpu.PrefetchScalarGridSpec(
        num_scalar_prefetch=3,
        grid=(len(pairs),),
        in_specs=[pl.BlockSpec((blk, width), lambda s, qi, kj, b: (qi[s], 0)),
                  pl.BlockSpec((blk, width), lambda s, qi, kj, b: (kj[s], 0)),
                  pl.BlockSpec((blk, width), lambda s, qi, kj, b: (kj[s], 0)),
                  pl.BlockSpec(excl.shape, lambda s, qi, kj, b: (0, 0))],
        out_specs=pl.BlockSpec((blk, width), lambda s, qi, kj, b: (qi[s], 0)),
        scratch_shapes=[pltpu.VMEM((n_heads, blk, LANES), F32),
                        pltpu.VMEM((n_heads, blk, head_dim), F32)],
    )
    return pl.pallas_call(
        functools.partial(_sb_prompt_kernel, blk=blk, n_heads=n_heads, head_dim=head_dim),
        grid_spec=grid_spec,
        out_shape=jax.ShapeDtypeStruct((rows, width), F32),
        compiler_params=pltpu.CompilerParams(
            dimension_semantics=("arbitrary",), vmem_limit_bytes=VMEM_LIMIT),
        name="sb_prompt",
    )(qi, kj, bias, q, k, v, excl)


def _sb_decode_kernel(pt_ref, bias_ref, q_ref, kn_ref, vn_ref, m_ref, m_new_ref, *rest,
                      pages_per_step, n_heads, head_dim, n_new):
    k_refs = rest[:pages_per_step]
    v_refs = rest[pages_per_step:2 * pages_per_step]
    o_ref = rest[2 * pages_per_step]
    carry_scr, acc_scr = rest[2 * pages_per_step + 1:]
    step = pl.program_id(1)
    n_rows = n_new * n_heads
    width = n_heads * head_dim
    page = m_ref.shape[0]

    row_head = lax.broadcasted_iota(jnp.int32, (n_heads, width), 0)
    lane_head = lax.broadcasted_iota(jnp.int32, (n_heads, width), 1) // head_dim
    head_mask = row_head == lane_head
    q = q_ref[0]
    q_bd = jnp.concatenate(
        [jnp.where(head_mask, jnp.broadcast_to(q[t:t + 1, :], (n_heads, width)), 0.0)
         for t in range(n_new)], axis=0).astype(BF16)
    rh = lax.broadcasted_iota(jnp.int32, (n_rows, 1), 0) % n_heads
    bias_col = jnp.zeros((n_rows, 1), F32)
    for h in range(n_heads):
        bias_col = jnp.where(rh == h, bias_ref[h], bias_col)

    @pl.when(step == 0)
    def _():
        pad = kn_ref.shape[1]
        kb = kn_ref[0].astype(BF16)
        vb = vn_ref[0].astype(BF16)
        zz = _dot_nt(q_bd, kb) + bias_col
        tq = lax.broadcasted_iota(jnp.int32, (n_rows, pad), 0) // n_heads
        tk = lax.broadcasted_iota(jnp.int32, (n_rows, pad), 1)
        causal = tk < tq
        sp = _softplus2(zz)
        fail = jnp.where(causal, sp, 0.0)
        rest_new = _dot(fail, m_new_ref[...], HIGHEST)
        wgt = jnp.where(causal, jnp.exp2(zz - sp + rest_new), 0.0)
        acc_scr[...] = _dot(wgt.astype(BF16), vb)
        carry_scr[...] = jnp.broadcast_to(-jnp.sum(fail, axis=-1, keepdims=True), carry_scr.shape)

    zs = [_dot(q_bd, k_refs[i][0].astype(BF16)) + bias_col for i in range(pages_per_step)]
    sps = [_softplus2(zz) for zz in zs]
    sums = _dot(jnp.concatenate([sp.astype(BF16) for sp in sps], axis=0), m_ref[...])
    acc = acc_scr[...]
    carry = carry_scr[...]
    for i in range(pages_per_step):
        s_i = sums[i * n_rows:(i + 1) * n_rows]
        rest_p = s_i[:, :page] + pltpu.repeat(carry, page // LANES, axis=1)
        wgt = jnp.exp2(zs[i] - sps[i] + rest_p)
        acc = acc + _dot_nt(wgt.astype(BF16), v_refs[i][0].astype(BF16))
        carry = carry + s_i[:, page:]
    acc_scr[...] = acc
    carry_scr[...] = carry

    @pl.when(step == pl.num_programs(1) - 1)
    def _():
        for t in range(n_new):
            blk = jnp.where(head_mask, acc[t * n_heads:(t + 1) * n_heads, :], 0.0)
            o_ref[0, t:t + 1, :] = jnp.sum(blk, axis=0, keepdims=True)


def _sb_decode(q, k_new, v_new, cache_kt, cache_vt, page_table, bias, pages_per_step, n_heads):
    n_seq, n_new, width = q.shape
    head_dim = width // n_heads
    n_pages = page_table.shape[1]
    page = cache_kt.shape[2]
    n_steps = n_pages // pages_per_step
    pad = k_new.shape[1]
    n_rows = n_new * n_heads
    excl = jnp.asarray(_suffix_matrix(page, True), BF16)
    excl_new = jnp.asarray(-np.tril(np.ones((pad, pad), np.float32), -1), F32)

    def page_spec(i):
        def index_map(b, s, pt, bias_):
            logical = n_pages - 1 - (s * pages_per_step + i)
            return (pt[b * n_pages + logical], 0, 0)
        return pl.BlockSpec((1, width, page), index_map)

    seq_spec = lambda rows: pl.BlockSpec((1, rows, width), lambda b, s, pt, bias_: (b, 0, 0))
    grid_spec = pltpu.PrefetchScalarGridSpec(
        num_scalar_prefetch=2,
        grid=(n_seq, n_steps),
        in_specs=[seq_spec(n_new), seq_spec(pad), seq_spec(pad),
                  pl.BlockSpec(excl.shape, lambda b, s, pt, bias_: (0, 0)),
                  pl.BlockSpec((pad, pad), lambda b, s, pt, bias_: (0, 0))]
                 + [page_spec(i) for i in range(pages_per_step)] * 2,
        out_specs=seq_spec(n_new),
        scratch_shapes=[pltpu.VMEM((n_rows, LANES), F32), pltpu.VMEM((n_rows, width), F32)],
    )
    return pl.pallas_call(
        functools.partial(_sb_decode_kernel, pages_per_step=pages_per_step, n_heads=n_heads,
                          head_dim=head_dim, n_new=n_new),
        grid_spec=grid_spec,
        out_shape=jax.ShapeDtypeStruct((n_seq, n_new, width), F32),
        compiler_params=pltpu.CompilerParams(
            dimension_semantics=("parallel", "arbitrary"), vmem_limit_bytes=VMEM_LIMIT),
        name="sb_decode",
    )(page_table.reshape(-1), bias, q, k_new, v_new, excl, excl_new,
      *([cache_kt] * pages_per_step), *([cache_vt] * pages_per_step))


def _merge_kernel(x_ref, orw_ref, osb_ref, zsb_ref, grw_ref, gsb_ref, gate_ref, gain_ref,
                  wrw_ref, wsb_ref, wout_ref, y_ref):
    o_sb = osb_ref[...] * _silu(zsb_ref[...])
    m = (_sigmoid(grw_ref[...]) * _dot(orw_ref[...].astype(BF16), wrw_ref[...])
         + _sigmoid(gsb_ref[...]) * _dot(o_sb.astype(BF16), wsb_ref[...]))
    u = _dot(m.astype(BF16), wout_ref[...])
    ms = jnp.mean(u * u, axis=-1, keepdims=True)
    y_ref[...] = x_ref[...] + gate_ref[...] * (u * lax.rsqrt(ms + RMS_EPS) * gain_ref[...])


def _merge(x, o_rw, o_sb, z_sb, g_rw, g_sb, gate, gain, w_rw, w_sb, w_out, tm):
    rows, d = x.shape
    width = o_rw.shape[1]
    per_row = gate.shape[0] != 1
    gate_spec = (pl.BlockSpec((tm, d), lambda i: (i, 0)) if per_row
                 else pl.BlockSpec((1, d), lambda i: (0, 0)))
    row_spec = lambda w: pl.BlockSpec((tm, w), lambda i: (i, 0))
    const = lambda arr: pl.BlockSpec(arr.shape, lambda i: (0, 0))
    return pl.pallas_call(
        _merge_kernel,
        grid=(rows // tm,),
        in_specs=[row_spec(d), row_spec(width), row_spec(width), row_spec(width),
                  row_spec(d), row_spec(d), gate_spec, const(gain),
                  const(w_rw), const(w_sb), const(w_out)],
        out_specs=row_spec(d),
        out_shape=jax.ShapeDtypeStruct((rows, d), F32),
        compiler_params=pltpu.CompilerParams(
            dimension_semantics=("parallel",), vmem_limit_bytes=VMEM_LIMIT),
        name="merge_out",
    )(x, o_rw, o_sb, z_sb, g_rw, g_sb, gate, gain, w_rw, w_sb, w_out)


def _row_tile(rows, want):
    t = min(rows, want)
    while rows % t:
        t //= 2
    return t


def kernel(x_prompt, x_sample, cache_sb_k, cache_sb_v, state_rwkv_wkv, state_rwkv_shift, page_table, c_prompt, c_sample, w_ada, b_ada, norm_pre, norm_post, w_in, mu_shift, w0_decay, w_decay_up, a0, w_a_up, k_k, k_a, r_k, ln_x_w, ln_x_b, sb_bias, w_branch_rwkv, w_branch_sb, w_out):
    bp, seq, d = x_prompt.shape
    bs, t_new, _ = x_sample.shape
    rw_heads, rw_dim = r_k.shape
    rw_width = rw_heads * rw_dim
    rw_cols = mu_shift.shape[0]
    n_pool, page, sb_heads, sb_dim = cache_sb_k.shape
    sb_width = sb_heads * sb_dim
    widths = (rw_cols, sb_width, sb_width, sb_width, sb_width, d, d)
    assert sum(widths) == w_in.shape[1]
    q_scale = float(sb_dim) ** -0.5 * LOG2_E
    bias2 = sb_bias * LOG2_E

    n_c = bp + bs
    c_all = jnp.concatenate([c_prompt, c_sample], axis=0)
    c_all = jnp.pad(c_all, ((0, (-n_c) % SUBLANES), (0, 0)))
    mod = _modulation(c_all, w_ada, b_ada)
    shift_all, scale_all, gate_all = mod[:, :d], mod[:, d:2 * d], mod[:, 2 * d:]

    w_in_bf = w_in.astype(BF16)
    w_rw_bf = w_branch_rwkv.astype(BF16)
    w_sb_bf = w_branch_sb.astype(BF16)
    w_out_bf = w_out.astype(BF16)
    gain_pre = norm_pre.reshape(1, d)
    gain_post = norm_post.reshape(1, d)
    row2 = lambda a: a.reshape(1, -1)
    rw_params = (row2(mu_shift), row2(w0_decay), w_decay_up, row2(a0), w_a_up, row2(k_k),
                 row2(k_a), row2(r_k), row2(ln_x_w), row2(ln_x_b))
    head_of = np.arange(rw_width) // rw_dim
    bd = jnp.asarray(head_of[:, None] == head_of[None, :], BF16)

    outs_p = []
    for b in range(bp):
        x = x_prompt[b]
        tm = _row_tile(seq, 256)
        p_rw, q, k, v, z, g_rw, g_sb = _project(
            x, scale_all[b:b + 1], shift_all[b:b + 1], gain_pre, w_in_bf, widths, q_scale, tm)
        chunk = _row_tile(seq, 64)
        o_rw, wkv = _rwkv(p_rw, jnp.zeros((1, 1, rw_cols), F32),
                          jnp.zeros((1, rw_heads, rw_dim, rw_dim), F32), rw_params, bd,
                          n_seq=1, chunk=chunk, t_valid=chunk, group=1)
        o_sb = _sb_prompt(q, k, v, bias2, _row_tile(seq, 256), sb_heads)
        y = _merge(x, o_rw, o_sb, z, g_rw, g_sb, gate_all[b:b + 1], gain_post,
                   w_rw_bf, w_sb_bf, w_out_bf, _row_tile(seq, 512))
        outs_p.append((y, k, v, wkv[0], p_rw[seq - 1]))
    y_p = jnp.stack([o[0] for o in outs_p])
    k_p = jnp.stack([o[1] for o in outs_p]).reshape(bp, seq, sb_heads, sb_dim)
    v_p = jnp.stack([o[2] for o in outs_p]).reshape(bp, seq, sb_heads, sb_dim)
    wkv_p = jnp.stack([o[3] for o in outs_p])
    shift_p = jnp.stack([o[4] for o in outs_p])

    rows_s = bs * t_new
    xs = x_sample.reshape(rows_s, d)
    expand = lambda a: jnp.repeat(a[bp:bp + bs], t_new, axis=0)
    tm = _row_tile(rows_s, 256)
    p_rw, q, k, v, z, g_rw, g_sb = _project(
        xs, expand(scale_all), expand(shift_all), gain_pre, w_in_bf, widths, q_scale, tm)
    t_pad = -(-t_new // BF16_ROWS) * BF16_ROWS
    pad_rows = lambda a: jnp.pad(a.reshape(bs, t_new, -1), ((0, 0), (0, t_pad - t_new), (0, 0)))
    o_rw, wkv_s = _rwkv(pad_rows(p_rw).reshape(bs * t_pad, rw_cols),
                        state_rwkv_shift.reshape(bs, 1, rw_cols), state_rwkv_wkv, rw_params, bd,
                        n_seq=bs, chunk=t_pad, t_valid=t_new, group=_row_tile(bs, 4))
    o_rw = o_rw.reshape(bs, t_pad, rw_width)[:, :t_new].reshape(rows_s, rw_width)
    n_pages = page_table.shape[1]
    pages_per_step = 8 if n_pages % 8 == 0 else 1
    to_kt = lambda c: jnp.transpose(c, (0, 2, 3, 1)).reshape(n_pool, sb_width, page)
    o_sb = _sb_decode(q.reshape(bs, t_new, sb_width), pad_rows(k), pad_rows(v),
                      to_kt(cache_sb_k), to_kt(cache_sb_v),
                      page_table, bias2, pages_per_step, sb_heads)
    y_s = _merge(xs, o_rw, o_sb.reshape(rows_s, sb_width), z, g_rw, g_sb, expand(gate_all),
                 gain_post, w_rw_bf, w_sb_bf, w_out_bf, _row_tile(rows_s, 512))
    y_s = y_s.reshape(bs, t_new, d)
    k_s = k.reshape(bs, t_new, sb_heads, sb_dim)
    v_s = v.reshape(bs, t_new, sb_heads, sb_dim)
    shift_s = p_rw.reshape(bs, t_new, rw_cols)[:, -1]
    return (y_p, y_s, k_p, v_p, k_s, v_s, wkv_p, wkv_s, shift_p, shift_s)
```

```python
import functools

import numpy as np
import jax
import jax.numpy as jnp
from jax import lax
from jax.experimental import pallas as pl
from jax.experimental.pallas import tpu as pltpu

F32 = jnp.float32
BF16 = jnp.bfloat16
HIGHEST = lax.Precision.HIGHEST

RMS_EPS = 1e-6
GN_EPS = 64e-5
NORM_EPS = 1e-12

VMEM_BYTES_V7X = 64 * 1024 * 1024
VMEM_LIMIT = 48 * 1024 * 1024
SUBLANES = 8
LANES = 128
BF16_ROWS = 16
LOG2_E = 1.4426950408889634

NT_DIMS = (((1,), (1,)), ((), ()))
TN_DIMS = (((0,), (0,)), ((), ()))


def _dot(a, b, precision=None):
    return jnp.dot(a, b, preferred_element_type=F32, precision=precision)


def _dot_nt(a, b, precision=None):
    return lax.dot_general(a, b, NT_DIMS, preferred_element_type=F32, precision=precision)


def _dot_tn(a, b, precision=None):
    return lax.dot_general(a, b, TN_DIMS, preferred_element_type=F32, precision=precision)


def _split2(x):
    hi = x.astype(BF16)
    lo = (x - hi.astype(F32)).astype(BF16)
    return hi, lo


def _split3(x):
    h1 = x.astype(BF16)
    r1 = x - h1.astype(F32)
    h2 = r1.astype(BF16)
    h3 = (r1 - h2.astype(F32)).astype(BF16)
    return h1, h2, h3


def _sigmoid(x):
    return 1.0 / (1.0 + jnp.exp(-x))


def _silu(x):
    return x * _sigmoid(x)


def _softplus(x):
    return jnp.maximum(x, 0.0) + jnp.log(1.0 + jnp.exp(-jnp.abs(x)))


def _mod_kernel(c_ref, w_ref, b_ref, o_ref):
    o_ref[...] = _dot(_silu(c_ref[...]), w_ref[...], HIGHEST) + b_ref[...]


def _modulation(c_all, w_ada, b_ada):
    rows, d = c_all.shape
    n_out = w_ada.shape[1]
    tn = d
    return pl.pallas_call(
        _mod_kernel,
        grid=(n_out // tn,),
        in_specs=[pl.BlockSpec((rows, d), lambda j: (0, 0)),
                  pl.BlockSpec((d, tn), lambda j: (0, j)),
                  pl.BlockSpec((1, tn), lambda j: (0, j))],
        out_specs=pl.BlockSpec((rows, tn), lambda j: (0, j)),
        out_shape=jax.ShapeDtypeStruct((rows, n_out), F32),
        compiler_params=pltpu.CompilerParams(vmem_limit_bytes=VMEM_LIMIT),
        name="adaln_mod",
    )(c_all, w_ada, b_ada.reshape(1, n_out))


def _proj_kernel(x_ref, scale_ref, shift_ref, gain_ref, w_ref,
                 prw_ref, q_ref, k_ref, v_ref, z_ref, grw_ref, gsb_ref, *, cols, q_scale):
    x = x_ref[...]
    ms = jnp.mean(x * x, axis=-1, keepdims=True)
    h = x * lax.rsqrt(ms + RMS_EPS) * gain_ref[...]
    h = h * (1.0 + scale_ref[...]) + shift_ref[...]
    hb = h.astype(BF16)
    outs = (prw_ref, q_ref, k_ref, v_ref, z_ref, grw_ref, gsb_ref)
    for idx, ref in enumerate(outs):
        r = _dot(hb, w_ref[:, cols[idx]:cols[idx + 1]])
        if idx == 1:
            r = r * q_scale
        ref[...] = r


def _project(x, scale, shift, gain, w_bf, widths, q_scale, tm):
    rows, d = x.shape
    cols = tuple(int(c) for c in np.cumsum((0,) + tuple(widths)))
    per_row = scale.shape[0] != 1
    mod_spec = (pl.BlockSpec((tm, d), lambda i: (i, 0)) if per_row
                else pl.BlockSpec((1, d), lambda i: (0, 0)))
    return pl.pallas_call(
        functools.partial(_proj_kernel, cols=cols, q_scale=q_scale),
        grid=(rows // tm,),
        in_specs=[pl.BlockSpec((tm, d), lambda i: (i, 0)), mod_spec, mod_spec,
                  pl.BlockSpec((1, d), lambda i: (0, 0)),
                  pl.BlockSpec(w_bf.shape, lambda i: (0, 0))],
        out_specs=[pl.BlockSpec((tm, w), lambda i: (i, 0)) for w in widths],
        out_shape=[jax.ShapeDtypeStruct((rows, w), F32) for w in widths],
        compiler_params=pltpu.CompilerParams(
            dimension_semantics=("parallel",), vmem_limit_bytes=VMEM_LIMIT),
        name="in_proj",
    )(x, scale, shift, gain, w_bf)


def _rwkv_kernel(p_ref, prev0_ref, s0_ref, mu_ref, w0_ref, wdu_ref, a0_ref, wau_ref,
                 kk_ref, ka_ref, rk_ref, lnw_ref, lnb_ref, bd_ref,
                 o_ref, sout_ref, s_scr, prev_scr, y_scr, *, chunk, t_valid, n_heads, head_dim, group):
    c_idx = pl.program_id(1)
    n_chunks = pl.num_programs(1)

    @pl.when(c_idx == 0)
    def _():
        for g in range(group):
            prev_scr[g] = jnp.broadcast_to(prev0_ref[g], prev_scr.shape[1:])
        s_scr[...] = s0_ref[...]

    consts = (mu_ref, w0_ref, wdu_ref, a0_ref, wau_ref, kk_ref, ka_ref, rk_ref, lnw_ref, lnb_ref, bd_ref)
    o_ref[...] = _rwkv_chunk(p_ref[...], consts, s_scr, prev_scr, y_scr, chunk=chunk, t_valid=t_valid,
                             n_heads=n_heads, head_dim=head_dim, group=group)

    @pl.when(c_idx == n_chunks - 1)
    def _():
        sout_ref[...] = s_scr[...]


def _rwkv_chunk(p, consts, s_scr, prev_scr, y_scr, *, chunk, t_valid, n_heads, head_dim, group):
    mu_ref, w0_ref, wdu_ref, a0_ref, wau_ref, kk_ref, ka_ref, rk_ref, lnw_ref, lnb_ref, bd_ref = consts
    width = n_heads * head_dim
    lora = wdu_ref.shape[0]
    n_rows = group * chunk
    row_all = lax.broadcasted_iota(jnp.int32, (n_rows, 1), 0)
    row = row_all % chunk
    prev = pltpu.roll(p, 1, axis=0)
    for g in range(group):
        prev = jnp.where(row_all == g * chunk, prev_scr[g, 0:1, :], prev)
        prev_scr[g] = jnp.broadcast_to(p[(g + 1) * chunk - 1:(g + 1) * chunk, :], prev_scr.shape[1:])
    m = p + (prev - p) * mu_ref[...]
    r = m[:, 0:width]
    k = m[:, width:2 * width]
    v = m[:, 2 * width:3 * width]
    z = m[:, 3 * width:4 * width]
    w_lo = m[:, 4 * width:4 * width + lora]
    a_lo = m[:, 4 * width + lora:4 * width + 2 * lora]

    bd = bd_ref[...]

    def head_sum(x):
        hi, lo = _split2(x)
        return _dot(hi, bd) + _dot(lo, bd)

    w_pre = w0_ref[...] + _dot(jnp.tanh(w_lo), wdu_ref[...], HIGHEST)
    log_w = -jnp.exp(-_softplus(-w_pre) - 0.5)
    a = _sigmoid(a0_ref[...] + _dot(a_lo, wau_ref[...], HIGHEST))
    kk = k * kk_ref[...]
    kk = kk / jnp.maximum(jnp.sqrt(head_sum(kk * kk)), NORM_EPS)
    k_mod = k * (1.0 + (a - 1.0) * ka_ref[...])
    if t_valid < chunk:
        live = row < t_valid
        log_w = jnp.where(live, log_w, 0.0)
        kk = jnp.where(live, kk, 0.0)
        k_mod = jnp.where(live, k_mod, 0.0)
        v = jnp.where(live, v, 0.0)

    ti = lax.broadcasted_iota(jnp.int32, (chunk, chunk), 0)
    si = lax.broadcasted_iota(jnp.int32, (chunk, chunk), 1)
    strict = ti > si
    lower = ti >= si
    eye = jnp.where(ti == si, 1.0, 0.0).astype(F32)
    ta = lax.broadcasted_iota(jnp.int32, (n_rows, n_rows), 0)
    sa = lax.broadcasted_iota(jnp.int32, (n_rows, n_rows), 1)
    tri = jnp.where((ta >= sa) & (ta // chunk == sa // chunk), 1.0, 0.0).astype(BF16)
    l1, l2, l3 = _split3(log_w)
    cum = _dot(tri, l1) + _dot(tri, l2) + _dot(tri, l3)
    g_t = jnp.exp(cum)
    g_inv = jnp.exp(-cum)
    at = -kk * jnp.exp(cum - log_w)
    bt = kk * a * g_inv
    kt = k_mod * g_inv
    rt = r * g_t

    n_double = max(int(np.ceil(np.log2(min(chunk, t_valid)))) - 1, 0)
    at_b, bt_b, kt_b, rt_b, v_b = (x.astype(BF16) for x in (at, bt, kt, rt, v))
    bt_lo = (bt - bt_b.astype(F32)).astype(BF16)
    kt_lo = (kt - kt_b.astype(F32)).astype(BF16)
    v_lo = (v - v_b.astype(F32)).astype(BF16)
    inst = [(g, h) for g in range(group) for h in range(n_heads)]
    ids = range(len(inst))
    cut = lambda x, g, h: x[g * chunk:(g + 1) * chunk, h * head_dim:(h + 1) * head_dim]
    s0 = [s_scr[g, h] for g, h in inst]
    ar = [jnp.concatenate([cut(at_b, g, h), cut(rt_b, g, h)], axis=0) for g, h in inst]
    g_b = [_dot_nt(ar[i], cut(bt_b, *inst[i])) for i in ids]
    g_k = [_dot_nt(ar[i], cut(kt_b, *inst[i])) for i in ids]
    from_state = [_dot_nt(ar[i], s0[i].astype(BF16)) for i in ids]
    l_ab = [jnp.where(strict, x[:chunk], 0.0) for x in g_b]
    l_ak = [jnp.where(strict, x[:chunk], 0.0).astype(BF16) for x in g_k]
    m_rb = [jnp.where(lower, x[chunk:], 0.0).astype(BF16) for x in g_b]
    m_rk = [jnp.where(lower, x[chunk:], 0.0).astype(BF16) for x in g_k]
    rhs = [from_state[i][:chunk] + _dot(l_ak[i], cut(v_b, *inst[i])) for i in ids]
    y_kv = [from_state[i][chunk:] + _dot(m_rk[i], cut(v_b, *inst[i])) for i in ids]
    t_inv = [eye + l for l in l_ab]
    pw_b = [l.astype(BF16) for l in l_ab]
    if n_double:
        pw_b = [_dot(x, x).astype(BF16) for x in pw_b]
    for j in range(n_double):
        t_b = [t.astype(BF16) for t in t_inv]
        t_inv = [t_inv[i] + _dot(t_b[i], pw_b[i]) for i in ids]
        if j + 1 < n_double:
            pw_b = [_dot(x, x).astype(BF16) for x in pw_b]
    t_b = [t.astype(BF16) for t in t_inv]
    u = [_dot(t_b[i], rhs[i].astype(BF16)) for i in ids]
    u_b = [x.astype(BF16) for x in u]
    u_lo = [(u[i] - u_b[i].astype(F32)).astype(BF16) for i in ids]
    l_hi = [l.astype(BF16) for l in l_ab]
    l_lo = [(l_ab[i] - l_hi[i].astype(F32)).astype(BF16) for i in ids]
    res = [rhs[i] - u[i] + (_dot(l_hi[i], u_b[i]) + _dot(l_hi[i], u_lo[i]) + _dot(l_lo[i], u_b[i]))
           for i in ids]
    u = [u[i] + _dot(t_b[i], res[i].astype(BF16)) for i in ids]
    u_b = [x.astype(BF16) for x in u]
    for i in ids:
        g, h = inst[i]
        y_scr[g * chunk:(g + 1) * chunk, h * head_dim:(h + 1) * head_dim] = y_kv[i] + _dot(m_rb[i], u_b[i])
        uv_hi = jnp.concatenate([u_b[i], cut(v_b, g, h)], axis=0)
        uv_lo = jnp.concatenate([(u[i] - u_b[i].astype(F32)).astype(BF16), cut(v_lo, g, h)], axis=0)
        bk_hi = jnp.concatenate([cut(bt_b, g, h), cut(kt_b, g, h)], axis=0)
        bk_lo = jnp.concatenate([cut(bt_lo, g, h), cut(kt_lo, g, h)], axis=0)
        upd = _dot_tn(uv_hi, bk_hi) + _dot_tn(uv_hi, bk_lo) + _dot_tn(uv_lo, bk_hi)
        g_end = g_t[(g + 1) * chunk - 1:(g + 1) * chunk, h * head_dim:(h + 1) * head_dim]
        s_scr[g, h] = (s0[i] + upd) * g_end

    y = y_scr[...]
    inv_n = 1.0 / head_dim
    mean = head_sum(y) * inv_n
    d = y - mean
    var = head_sum(d * d) * inv_n
    yn = d * lax.rsqrt(var + GN_EPS) * lnw_ref[...] + lnb_ref[...]
    bonus = head_sum(r * k_mod * rk_ref[...])
    return (yn + bonus * v) * _silu(z)


def _rwkv(p_rw, prev0, s0, params, bd, *, n_seq, chunk, t_valid, group):
    rows, cols = p_rw.shape
    n_chunks = rows // (n_seq * chunk)
    assert group == 1 or n_chunks == 1
    n_heads, head_dim = s0.shape[1], s0.shape[2]
    width = n_heads * head_dim
    const = lambda arr: pl.BlockSpec(arr.shape, lambda s, c: (0,) * arr.ndim)
    state_spec = pl.BlockSpec((group, n_heads, head_dim, head_dim), lambda s, c: (s, 0, 0, 0))
    return pl.pallas_call(
        functools.partial(_rwkv_kernel, chunk=chunk, t_valid=t_valid,
                          n_heads=n_heads, head_dim=head_dim, group=group),
        grid=(n_seq // group, n_chunks),
        in_specs=[pl.BlockSpec((group * chunk, cols), lambda s, c: (s * n_chunks + c, 0)),
                  pl.BlockSpec((group, 1, cols), lambda s, c: (s, 0, 0)),
                  state_spec]
                 + [const(a) for a in params] + [const(bd)],
        out_specs=[pl.BlockSpec((group * chunk, width), lambda s, c: (s * n_chunks + c, 0)),
                   state_spec],
        out_shape=[jax.ShapeDtypeStruct((rows, width), F32),
                   jax.ShapeDtypeStruct(s0.shape, F32)],
        scratch_shapes=[pltpu.VMEM((group, n_heads, head_dim, head_dim), F32),
                        pltpu.VMEM((group, SUBLANES, cols), F32),
                        pltpu.VMEM((group * chunk, width), F32)],
        compiler_params=pltpu.CompilerParams(
            dimension_semantics=("parallel", "arbitrary"), vmem_limit_bytes=VMEM_LIMIT),
        name="rwkv7_chunk",
    )(p_rw, prev0, s0, *params, bd)


SOFTPLUS2_LINEAR_ABOVE = 100.0


def _softplus2(x):
    return jnp.where(x > SOFTPLUS2_LINEAR_ABOVE, x, jnp.log2(1.0 + jnp.exp2(x)))


def _suffix_matrix(n, with_total):
    m = -np.tril(np.ones((n, n), np.float32), -1)
    return np.concatenate([m, -np.ones((n, LANES), np.float32)], axis=1) if with_total else m


def _sb_prompt_kernel(qi_ref, kj_ref, bias_ref, q_ref, k_ref, v_ref, m_ref, o_ref,
                      carry_scr, acc_scr, *, blk, n_sub, n_heads, head_dim):
    step = pl.program_id(0)
    qi = qi_ref[step]
    kj = kj_ref[step]
    lanes = carry_scr.shape[-1]
    FULL, DIAG = "full", "diag"

    def sweep(modes):
        kb = k_ref[...].astype(BF16)
        vb = v_ref[...].astype(BF16)
        qb = {s: q_ref[s * blk:(s + 1) * blk, :].astype(BF16) for s in range(n_sub) if modes[s]}
        if DIAG in modes:
            ti = lax.broadcasted_iota(jnp.int32, (blk, blk), 0)
            si = lax.broadcasted_iota(jnp.int32, (blk, blk), 1)
            causal = si < ti
        sl = lambda h: slice(h * head_dim, (h + 1) * head_dim)
        inst = [(s, h) for s in range(n_sub) if modes[s] for h in range(n_heads)]
        zz, log_beta, sums, first = {}, {}, {}, {}
        for i in range(len(inst) + 2):
            if i < len(inst):
                s, h = inst[i]
                zz[i] = _dot_nt(qb[s][:, sl(h)], kb[:, sl(h)]) + bias_ref[h]
            n = i - 1
            if 0 <= n < len(inst):
                diagonal = modes[inst[n][0]] == DIAG
                z_n = zz.pop(n)
                sp = _softplus2(z_n)
                log_beta[n] = z_n - sp
                fail = jnp.where(causal, sp, 0.0) if diagonal else sp
                first[n] = fail[:, 0:1]
                sums[n] = _dot(fail.astype(BF16), m_ref[...])
            n = i - 2
            if 0 <= n < len(inst):
                s, h = inst[n]
                diagonal = modes[s] == DIAG
                rest = sums.pop(n)
                total = jnp.broadcast_to(rest[:, 0:1] - first.pop(n), (blk, lanes))
                wgt = jnp.exp2(log_beta.pop(n) + rest)
                if diagonal:
                    wgt = jnp.where(causal, wgt, 0.0)
                contrib = _dot(wgt.astype(BF16), vb[:, sl(h)])
                if diagonal:
                    carry_scr[s, h] = total
                    acc_scr[s, h] = contrib
                else:
                    carry = carry_scr[s, h]
                    acc_scr[s, h] = acc_scr[s, h] + contrib * jnp.exp2(carry[:, :head_dim])
                    carry_scr[s, h] = carry + total

    for first_live in range(n_sub):
        modes = tuple(None if s < first_live else (DIAG if s == first_live else FULL) for s in range(n_sub))

        @pl.when(kj == qi * n_sub + first_live)
        def _(modes=modes):
            sweep(modes)

    @pl.when(kj < qi * n_sub)
    def _():
        sweep((FULL,) * n_sub)

    @pl.when(kj == 0)
    def _():
        for s in range(n_sub):
            for h in range(n_heads):
                o_ref[s * blk:(s + 1) * blk, h * head_dim:(h + 1) * head_dim] = acc_scr[s, h]


def _sb_prompt(q, k, v, bias, blk, n_sub, n_heads):
    rows, width = q.shape
    head_dim = width // n_heads
    nq = rows // (blk * n_sub)
    pairs = [(i, j) for i in range(nq) for j in range(i * n_sub + n_sub - 1, -1, -1)]
    qi = jnp.asarray([p[0] for p in pairs], jnp.int32)
    kj = jnp.asarray([p[1] for p in pairs], jnp.int32)
    excl = jnp.asarray(_suffix_matrix(blk, False), BF16)
    grid_spec = pltpu.PrefetchScalarGridSpec(
        num_scalar_prefetch=3,
        grid=(len(pairs),),
        in_specs=[pl.BlockSpec((blk * n_sub, width), lambda s, qi, kj, b: (qi[s], 0)),
                  pl.BlockSpec((blk, width), lambda s, qi, kj, b: (kj[s], 0)),
                  pl.BlockSpec((blk, width), lambda s, qi, kj, b: (kj[s], 0)),
                  pl.BlockSpec(excl.shape, lambda s, qi, kj, b: (0, 0))],
        out_specs=pl.BlockSpec((blk * n_sub, width), lambda s, qi, kj, b: (qi[s], 0)),
        scratch_shapes=[pltpu.VMEM((n_sub, n_heads, blk, LANES), F32),
                        pltpu.VMEM((n_sub, n_heads, blk, head_dim), F32)],
    )
    return pl.pallas_call(
        functools.partial(_sb_prompt_kernel, blk=blk, n_sub=n_sub, n_heads=n_heads, head_dim=head_dim),
        grid_spec=grid_spec,
        out_shape=jax.ShapeDtypeStruct((rows, width), F32),
        compiler_params=pltpu.CompilerParams(
            dimension_semantics=("arbitrary",), vmem_limit_bytes=VMEM_LIMIT),
        name="sb_prompt",
    )(qi, kj, bias, q, k, v, excl)


def _sb_decode_kernel(pt_ref, bias_ref, q_ref, kn_ref, vn_ref, m_ref, m_new_ref, *rest,
                      pages_per_step, n_heads, head_dim, n_new):
    k_refs = rest[:pages_per_step]
    v_refs = rest[pages_per_step:2 * pages_per_step]
    o_ref = rest[2 * pages_per_step]
    carry_scr, acc_scr = rest[2 * pages_per_step + 1:]
    step = pl.program_id(1)
    n_rows = n_new * n_heads
    width = n_heads * head_dim
    page = m_ref.shape[0]

    row_head = lax.broadcasted_iota(jnp.int32, (n_heads, width), 0)
    lane_head = lax.broadcasted_iota(jnp.int32, (n_heads, width), 1) // head_dim
    head_mask = row_head == lane_head
    q = q_ref[0]
    q_bd = jnp.concatenate(
        [jnp.where(head_mask, jnp.broadcast_to(q[t:t + 1, :], (n_heads, width)), 0.0)
         for t in range(n_new)], axis=0).astype(BF16)
    rh = lax.broadcasted_iota(jnp.int32, (n_rows, 1), 0) % n_heads
    bias_col = jnp.zeros((n_rows, 1), F32)
    for h in range(n_heads):
        bias_col = jnp.where(rh == h, bias_ref[h], bias_col)

    @pl.when(step == 0)
    def _():
        pad = kn_ref.shape[1]
        kb = kn_ref[0].astype(BF16)
        vb = vn_ref[0].astype(BF16)
        zz = _dot_nt(q_bd, kb) + bias_col
        tq = lax.broadcasted_iota(jnp.int32, (n_rows, pad), 0) // n_heads
        tk = lax.broadcasted_iota(jnp.int32, (n_rows, pad), 1)
        causal = tk < tq
        sp = _softplus2(zz)
        fail = jnp.where(causal, sp, 0.0)
        rest_new = _dot(fail, m_new_ref[...], HIGHEST)
        wgt = jnp.where(causal, jnp.exp2(zz - sp + rest_new), 0.0)
        acc_scr[...] = _dot(wgt.astype(BF16), vb)
        carry_scr[...] = jnp.broadcast_to(-jnp.sum(fail, axis=-1, keepdims=True), carry_scr.shape)

    zs = [_dot(q_bd, k_refs[i][0].astype(BF16)) + bias_col for i in range(pages_per_step)]
    sps = [_softplus2(zz) for zz in zs]
    sums = _dot(jnp.concatenate([sp.astype(BF16) for sp in sps], axis=0), m_ref[...])
    acc = acc_scr[...]
    carry = carry_scr[...]
    for i in range(pages_per_step):
        s_i = sums[i * n_rows:(i + 1) * n_rows]
        rest_p = s_i[:, :page] + pltpu.repeat(carry, page // LANES, axis=1)
        wgt = jnp.exp2(zs[i] - sps[i] + rest_p)
        acc = acc + _dot_nt(wgt.astype(BF16), v_refs[i][0].astype(BF16))
        carry = carry + s_i[:, page:]
    acc_scr[...] = acc
    carry_scr[...] = carry

    @pl.when(step == pl.num_programs(1) - 1)
    def _():
        for t in range(n_new):
            blk = jnp.where(head_mask, acc[t * n_heads:(t + 1) * n_heads, :], 0.0)
            o_ref[0, t:t + 1, :] = jnp.sum(blk, axis=0, keepdims=True)


def _sb_decode(q, k_new, v_new, cache_kt, cache_vt, page_table, bias, pages_per_step, n_heads):
    n_seq, n_new, width = q.shape
    head_dim = width // n_heads
    n_pages = page_table.shape[1]
    page = cache_kt.shape[2]
    n_steps = n_pages // pages_per_step
    pad = k_new.shape[1]
    n_rows = n_new * n_heads
    excl = jnp.asarray(_suffix_matrix(page, True), BF16)
    excl_new = jnp.asarray(-np.tril(np.ones((pad, pad), np.float32), -1), F32)

    def page_spec(i):
        def index_map(b, s, pt, bias_):
            logical = n_pages - 1 - (s * pages_per_step + i)
            return (pt[b * n_pages + logical], 0, 0)
        return pl.BlockSpec((1, width, page), index_map)

    seq_spec = lambda rows: pl.BlockSpec((1, rows, width), lambda b, s, pt, bias_: (b, 0, 0))
    grid_spec = pltpu.PrefetchScalarGridSpec(
        num_scalar_prefetch=2,
        grid=(n_seq, n_steps),
        in_specs=[seq_spec(n_new), seq_spec(pad), seq_spec(pad),
                  pl.BlockSpec(excl.shape, lambda b, s, pt, bias_: (0, 0)),
                  pl.BlockSpec((pad, pad), lambda b, s, pt, bias_: (0, 0))]
                 + [page_spec(i) for i in range(pages_per_step)] * 2,
        out_specs=seq_spec(n_new),
        scratch_shapes=[pltpu.VMEM((n_rows, LANES), F32), pltpu.VMEM((n_rows, width), F32)],
    )
    return pl.pallas_call(
        functools.partial(_sb_decode_kernel, pages_per_step=pages_per_step, n_heads=n_heads,
                          head_dim=head_dim, n_new=n_new),
        grid_spec=grid_spec,
        out_shape=jax.ShapeDtypeStruct((n_seq, n_new, width), F32),
        compiler_params=pltpu.CompilerParams(
            dimension_semantics=("parallel", "arbitrary"), vmem_limit_bytes=VMEM_LIMIT),
        name="sb_decode",
    )(page_table.reshape(-1), bias, q, k_new, v_new, excl, excl_new,
      *([cache_kt] * pages_per_step), *([cache_vt] * pages_per_step))


def _merge_kernel(x_ref, orw_ref, osb_ref, zsb_ref, grw_ref, gsb_ref, gate_ref, gain_ref,
                  wrw_ref, wsb_ref, wout_ref, y_ref):
    o_sb = osb_ref[...] * _silu(zsb_ref[...])
    m = (_sigmoid(grw_ref[...]) * _dot(orw_ref[...].astype(BF16), wrw_ref[...])
         + _sigmoid(gsb_ref[...]) * _dot(o_sb.astype(BF16), wsb_ref[...]))
    u = _dot(m.astype(BF16), wout_ref[...])
    ms = jnp.mean(u * u, axis=-1, keepdims=True)
    y_ref[...] = x_ref[...] + gate_ref[...] * (u * lax.rsqrt(ms + RMS_EPS) * gain_ref[...])


def _merge(x, o_rw, o_sb, z_sb, g_rw, g_sb, gate, gain, w_rw, w_sb, w_out, tm):
    rows, d = x.shape
    width = o_rw.shape[1]
    per_row = gate.shape[0] != 1
    gate_spec = (pl.BlockSpec((tm, d), lambda i: (i, 0)) if per_row
                 else pl.BlockSpec((1, d), lambda i: (0, 0)))
    row_spec = lambda w: pl.BlockSpec((tm, w), lambda i: (i, 0))
    const = lambda arr: pl.BlockSpec(arr.shape, lambda i: (0, 0))
    return pl.pallas_call(
        _merge_kernel,
        grid=(rows // tm,),
        in_specs=[row_spec(d), row_spec(width), row_spec(width), row_spec(width),
                  row_spec(d), row_spec(d), gate_spec, const(gain),
                  const(w_rw), const(w_sb), const(w_out)],
        out_specs=row_spec(d),
        out_shape=jax.ShapeDtypeStruct((rows, d), F32),
        compiler_params=pltpu.CompilerParams(
            dimension_semantics=("parallel",), vmem_limit_bytes=VMEM_LIMIT),
        name="merge_out",
    )(x, o_rw, o_sb, z_sb, g_rw, g_sb, gate, gain, w_rw, w_sb, w_out)


def _row_tile(rows, want):
    t = min(rows, want)
    while rows % t:
        t //= 2
    return t


def kernel(x_prompt, x_sample, cache_sb_k, cache_sb_v, state_rwkv_wkv, state_rwkv_shift, page_table, c_prompt, c_sample, w_ada, b_ada, norm_pre, norm_post, w_in, mu_shift, w0_decay, w_decay_up, a0, w_a_up, k_k, k_a, r_k, ln_x_w, ln_x_b, sb_bias, w_branch_rwkv, w_branch_sb, w_out):
    bp, seq, d = x_prompt.shape
    bs, t_new, _ = x_sample.shape
    rw_heads, rw_dim = r_k.shape
    rw_width = rw_heads * rw_dim
    rw_cols = mu_shift.shape[0]
    n_pool, page, sb_heads, sb_dim = cache_sb_k.shape
    sb_width = sb_heads * sb_dim
    widths = (rw_cols, sb_width, sb_width, sb_width, sb_width, d, d)
    assert sum(widths) == w_in.shape[1]
    q_scale = float(sb_dim) ** -0.5 * LOG2_E
    bias2 = sb_bias * LOG2_E

    n_c = bp + bs
    c_all = jnp.concatenate([c_prompt, c_sample], axis=0)
    c_all = jnp.pad(c_all, ((0, (-n_c) % SUBLANES), (0, 0)))
    mod = _modulation(c_all, w_ada, b_ada)
    shift_all, scale_all, gate_all = mod[:, :d], mod[:, d:2 * d], mod[:, 2 * d:]

    w_in_bf = w_in.astype(BF16)
    w_rw_bf = w_branch_rwkv.astype(BF16)
    w_sb_bf = w_branch_sb.astype(BF16)
    w_out_bf = w_out.astype(BF16)
    gain_pre = norm_pre.reshape(1, d)
    gain_post = norm_post.reshape(1, d)
    row2 = lambda a: a.reshape(1, -1)
    rw_params = (row2(mu_shift), row2(w0_decay), w_decay_up, row2(a0), w_a_up, row2(k_k),
                 row2(k_a), row2(r_k), row2(ln_x_w), row2(ln_x_b))
    head_of = np.arange(rw_width) // rw_dim
    bd = jnp.asarray(head_of[:, None] == head_of[None, :], BF16)

    outs_p = []
    for b in range(bp):
        x = x_prompt[b]
        tm = _row_tile(seq, 256)
        p_rw, q, k, v, z, g_rw, g_sb = _project(
            x, scale_all[b:b + 1], shift_all[b:b + 1], gain_pre, w_in_bf, widths, q_scale, tm)
        chunk = _row_tile(seq, 128)
        o_rw, wkv = _rwkv(p_rw, jnp.zeros((1, 1, rw_cols), F32),
                          jnp.zeros((1, rw_heads, rw_dim, rw_dim), F32), rw_params, bd,
                          n_seq=1, chunk=chunk, t_valid=chunk, group=1)
        sb_blk = _row_tile(seq, 256)
        o_sb = _sb_prompt(q, k, v, bias2, sb_blk, 2 if seq % (2 * sb_blk) == 0 else 1, sb_heads)
        y = _merge(x, o_rw, o_sb, z, g_rw, g_sb, gate_all[b:b + 1], gain_post,
                   w_rw_bf, w_sb_bf, w_out_bf, _row_tile(seq, 512))
        outs_p.append((y, k, v, wkv[0], p_rw[seq - 1]))
    y_p = jnp.stack([o[0] for o in outs_p])
    k_p = jnp.stack([o[1] for o in outs_p]).reshape(bp, seq, sb_heads, sb_dim)
    v_p = jnp.stack([o[2] for o in outs_p]).reshape(bp, seq, sb_heads, sb_dim)
    wkv_p = jnp.stack([o[3] for o in outs_p])
    shift_p = jnp.stack([o[4] for o in outs_p])

    rows_s = bs * t_new
    xs = x_sample.reshape(rows_s, d)
    expand = lambda a: jnp.repeat(a[bp:bp + bs], t_new, axis=0)
    tm = _row_tile(rows_s, 256)
    p_rw, q, k, v, z, g_rw, g_sb = _project(
        xs, expand(scale_all), expand(shift_all), gain_pre, w_in_bf, widths, q_scale, tm)
    t_pad = -(-t_new // BF16_ROWS) * BF16_ROWS
    pad_rows = lambda a: jnp.pad(a.reshape(bs, t_new, -1), ((0, 0), (0, t_pad - t_new), (0, 0)))
    o_rw, wkv_s = _rwkv(pad_rows(p_rw).reshape(bs * t_pad, rw_cols),
                        state_rwkv_shift.reshape(bs, 1, rw_cols), state_rwkv_wkv, rw_params, bd,
                        n_seq=bs, chunk=t_pad, t_valid=t_new, group=_row_tile(bs, 4))
    o_rw = o_rw.reshape(bs, t_pad, rw_width)[:, :t_new].reshape(rows_s, rw_width)
    n_pages = page_table.shape[1]
    pages_per_step = next(p for p in (16, 8, 4, 2, 1) if n_pages % p == 0)
    to_kt = lambda c: jnp.transpose(c, (0, 2, 3, 1)).reshape(n_pool, sb_width, page)
    o_sb = _sb_decode(q.reshape(bs, t_new, sb_width), pad_rows(k), pad_rows(v),
                      to_kt(cache_sb_k), to_kt(cache_sb_v),
                      page_table, bias2, pages_per_step, sb_heads)
    y_s = _merge(xs, o_rw, o_sb.reshape(rows_s, sb_width), z, g_rw, g_sb, expand(gate_all),
                 gain_post, w_rw_bf, w_sb_bf, w_out_bf, _row_tile(rows_s, 512))
    y_s = y_s.reshape(bs, t_new, d)
    k_s = k.reshape(bs, t_new, sb_heads, sb_dim)
    v_s = v.reshape(bs, t_new, sb_heads, sb_dim)
    shift_s = p_rw.reshape(bs, t_new, rw_cols)[:, -1]
    return (y_p, y_s, k_p, v_p, k_s, v_s, wkv_p, wkv_s, shift_p, shift_s)
```

```python
import functools

import numpy as np
import jax
import jax.numpy as jnp
from jax import lax
from jax.experimental import pallas as pl
from jax.experimental.pallas import tpu as pltpu

F32 = jnp.float32
BF16 = jnp.bfloat16
HIGHEST = lax.Precision.HIGHEST

RMS_EPS = 1e-6
GN_EPS = 64e-5
NORM_EPS = 1e-12

VMEM_BYTES_V7X = 64 * 1024 * 1024
VMEM_LIMIT = 48 * 1024 * 1024
SUBLANES = 8
LANES = 128
BF16_ROWS = 16
LOG2_E = 1.4426950408889634

NT_DIMS = (((1,), (1,)), ((), ()))
TN_DIMS = (((0,), (0,)), ((), ()))


def _dot(a, b, precision=None):
    return jnp.dot(a, b, preferred_element_type=F32, precision=precision)


def _dot_nt(a, b, precision=None):
    return lax.dot_general(a, b, NT_DIMS, preferred_element_type=F32, precision=precision)


def _dot_tn(a, b, precision=None):
    return lax.dot_general(a, b, TN_DIMS, preferred_element_type=F32, precision=precision)


def _split2(x):
    hi = x.astype(BF16)
    lo = (x - hi.astype(F32)).astype(BF16)
    return hi, lo


def _split3(x):
    h1 = x.astype(BF16)
    r1 = x - h1.astype(F32)
    h2 = r1.astype(BF16)
    h3 = (r1 - h2.astype(F32)).astype(BF16)
    return h1, h2, h3


def _sigmoid(x):
    return 1.0 / (1.0 + jnp.exp(-x))


def _silu(x):
    return x * _sigmoid(x)


def _softplus(x):
    return jnp.maximum(x, 0.0) + jnp.log(1.0 + jnp.exp(-jnp.abs(x)))


def _mod_kernel(c_ref, w_ref, b_ref, o_ref):
    o_ref[...] = _dot(_silu(c_ref[...]), w_ref[...], HIGHEST) + b_ref[...]


def _modulation(c_all, w_ada, b_ada):
    rows, d = c_all.shape
    n_out = w_ada.shape[1]
    tn = d
    return pl.pallas_call(
        _mod_kernel,
        grid=(n_out // tn,),
        in_specs=[pl.BlockSpec((rows, d), lambda j: (0, 0)),
                  pl.BlockSpec((d, tn), lambda j: (0, j)),
                  pl.BlockSpec((1, tn), lambda j: (0, j))],
        out_specs=pl.BlockSpec((rows, tn), lambda j: (0, j)),
        out_shape=jax.ShapeDtypeStruct((rows, n_out), F32),
        compiler_params=pltpu.CompilerParams(vmem_limit_bytes=VMEM_LIMIT),
        name="adaln_mod",
    )(c_all, w_ada, b_ada.reshape(1, n_out))


def _proj_kernel(x_ref, scale_ref, shift_ref, gain_ref, w_ref,
                 prw_ref, q_ref, k_ref, v_ref, z_ref, grw_ref, gsb_ref, *, cols, q_scale):
    x = x_ref[...]
    ms = jnp.mean(x * x, axis=-1, keepdims=True)
    h = x * lax.rsqrt(ms + RMS_EPS) * gain_ref[...]
    h = h * (1.0 + scale_ref[...]) + shift_ref[...]
    hb = h.astype(BF16)
    outs = (prw_ref, q_ref, k_ref, v_ref, z_ref, grw_ref, gsb_ref)
    for idx, ref in enumerate(outs):
        r = _dot(hb, w_ref[:, cols[idx]:cols[idx + 1]])
        if idx == 1:
            r = r * q_scale
        ref[...] = r


def _project(x, scale, shift, gain, w_bf, widths, q_scale, tm):
    rows, d = x.shape
    cols = tuple(int(c) for c in np.cumsum((0,) + tuple(widths)))
    per_row = scale.shape[0] != 1
    mod_spec = (pl.BlockSpec((tm, d), lambda i: (i, 0)) if per_row
                else pl.BlockSpec((1, d), lambda i: (0, 0)))
    return pl.pallas_call(
        functools.partial(_proj_kernel, cols=cols, q_scale=q_scale),
        grid=(rows // tm,),
        in_specs=[pl.BlockSpec((tm, d), lambda i: (i, 0)), mod_spec, mod_spec,
                  pl.BlockSpec((1, d), lambda i: (0, 0)),
                  pl.BlockSpec(w_bf.shape, lambda i: (0, 0))],
        out_specs=[pl.BlockSpec((tm, w), lambda i: (i, 0)) for w in widths],
        out_shape=[jax.ShapeDtypeStruct((rows, w), F32) for w in widths],
        compiler_params=pltpu.CompilerParams(
            dimension_semantics=("parallel",), vmem_limit_bytes=VMEM_LIMIT),
        name="in_proj",
    )(x, scale, shift, gain, w_bf)


def _rwkv_kernel(p_ref, prev0_ref, s0_ref, mu_ref, w0_ref, wdu_ref, a0_ref, wau_ref,
                 kk_ref, ka_ref, rk_ref, lnw_ref, lnb_ref, bd_ref,
                 o_ref, sout_ref, s_scr, prev_scr, y_scr, *, chunk, t_valid, n_heads, head_dim, group):
    c_idx = pl.program_id(1)
    n_chunks = pl.num_programs(1)

    @pl.when(c_idx == 0)
    def _():
        for g in range(group):
            prev_scr[g] = jnp.broadcast_to(prev0_ref[g], prev_scr.shape[1:])
        s_scr[...] = s0_ref[...]

    consts = (mu_ref, w0_ref, wdu_ref, a0_ref, wau_ref, kk_ref, ka_ref, rk_ref, lnw_ref, lnb_ref, bd_ref)
    o_ref[...] = _rwkv_chunk(p_ref[...], consts, s_scr, prev_scr, y_scr, chunk=chunk, t_valid=t_valid,
                             n_heads=n_heads, head_dim=head_dim, group=group)

    @pl.when(c_idx == n_chunks - 1)
    def _():
        sout_ref[...] = s_scr[...]


def _rwkv_chunk(p, consts, s_scr, prev_scr, y_scr, *, chunk, t_valid, n_heads, head_dim, group):
    mu_ref, w0_ref, wdu_ref, a0_ref, wau_ref, kk_ref, ka_ref, rk_ref, lnw_ref, lnb_ref, bd_ref = consts
    width = n_heads * head_dim
    lora = wdu_ref.shape[0]
    n_rows = group * chunk
    row_all = lax.broadcasted_iota(jnp.int32, (n_rows, 1), 0)
    row = row_all % chunk
    prev = pltpu.roll(p, 1, axis=0)
    for g in range(group):
        prev = jnp.where(row_all == g * chunk, prev_scr[g, 0:1, :], prev)
        prev_scr[g] = jnp.broadcast_to(p[(g + 1) * chunk - 1:(g + 1) * chunk, :], prev_scr.shape[1:])
    m = p + (prev - p) * mu_ref[...]
    r = m[:, 0:width]
    k = m[:, width:2 * width]
    v = m[:, 2 * width:3 * width]
    z = m[:, 3 * width:4 * width]
    w_lo = m[:, 4 * width:4 * width + lora]
    a_lo = m[:, 4 * width + lora:4 * width + 2 * lora]

    bd = bd_ref[...]

    def head_sum(x):
        hi, lo = _split2(x)
        return _dot(hi, bd) + _dot(lo, bd)

    w_pre = w0_ref[...] + _dot(jnp.tanh(w_lo), wdu_ref[...], HIGHEST)
    log_w = -jnp.exp(-_softplus(-w_pre) - 0.5)
    a = _sigmoid(a0_ref[...] + _dot(a_lo, wau_ref[...], HIGHEST))
    kk = k * kk_ref[...]
    kk = kk / jnp.maximum(jnp.sqrt(head_sum(kk * kk)), NORM_EPS)
    k_mod = k * (1.0 + (a - 1.0) * ka_ref[...])
    if t_valid < chunk:
        live = row < t_valid
        log_w = jnp.where(live, log_w, 0.0)
        kk = jnp.where(live, kk, 0.0)
        k_mod = jnp.where(live, k_mod, 0.0)
        v = jnp.where(live, v, 0.0)

    ti = lax.broadcasted_iota(jnp.int32, (chunk, chunk), 0)
    si = lax.broadcasted_iota(jnp.int32, (chunk, chunk), 1)
    strict = ti > si
    lower = ti >= si
    eye = jnp.where(ti == si, 1.0, 0.0).astype(F32)
    ta = lax.broadcasted_iota(jnp.int32, (n_rows, n_rows), 0)
    sa = lax.broadcasted_iota(jnp.int32, (n_rows, n_rows), 1)
    tri = jnp.where((ta >= sa) & (ta // chunk == sa // chunk), 1.0, 0.0).astype(BF16)
    l1, l2, l3 = _split3(log_w)
    cum = _dot(tri, l1) + _dot(tri, l2) + _dot(tri, l3)
    g_t = jnp.exp(cum)
    g_inv = jnp.exp(-cum)
    at = -kk * jnp.exp(cum - log_w)
    bt = kk * a * g_inv
    kt = k_mod * g_inv
    rt = r * g_t

    n_double = max(int(np.ceil(np.log2(min(chunk, t_valid)))) - 1, 0)
    at_b, bt_b, kt_b, rt_b, v_b = (x.astype(BF16) for x in (at, bt, kt, rt, v))
    bt_lo = (bt - bt_b.astype(F32)).astype(BF16)
    kt_lo = (kt - kt_b.astype(F32)).astype(BF16)
    v_lo = (v - v_b.astype(F32)).astype(BF16)
    inst = [(g, h) for g in range(group) for h in range(n_heads)]
    ids = range(len(inst))
    cut = lambda x, g, h: x[g * chunk:(g + 1) * chunk, h * head_dim:(h + 1) * head_dim]
    s0 = [s_scr[g, h] for g, h in inst]
    ar = [jnp.concatenate([cut(at_b, g, h), cut(rt_b, g, h)], axis=0) for g, h in inst]
    g_b = [_dot_nt(ar[i], cut(bt_b, *inst[i])) for i in ids]
    g_k = [_dot_nt(ar[i], cut(kt_b, *inst[i])) for i in ids]
    from_state = [_dot_nt(ar[i], s0[i].astype(BF16)) for i in ids]
    l_ab = [jnp.where(strict, x[:chunk], 0.0) for x in g_b]
    l_ak = [jnp.where(strict, x[:chunk], 0.0).astype(BF16) for x in g_k]
    m_rb = [jnp.where(lower, x[chunk:], 0.0).astype(BF16) for x in g_b]
    m_rk = [jnp.where(lower, x[chunk:], 0.0).astype(BF16) for x in g_k]
    rhs = [from_state[i][:chunk] + _dot(l_ak[i], cut(v_b, *inst[i])) for i in ids]
    y_kv = [from_state[i][chunk:] + _dot(m_rk[i], cut(v_b, *inst[i])) for i in ids]
    t_inv = [eye + l for l in l_ab]
    pw_b = [l.astype(BF16) for l in l_ab]
    if n_double:
        pw_b = [_dot(x, x).astype(BF16) for x in pw_b]
    for j in range(n_double):
        t_b = [t.astype(BF16) for t in t_inv]
        t_inv = [t_inv[i] + _dot(t_b[i], pw_b[i]) for i in ids]
        if j + 1 < n_double:
            pw_b = [_dot(x, x).astype(BF16) for x in pw_b]
    t_b = [t.astype(BF16) for t in t_inv]
    u = [_dot(t_b[i], rhs[i].astype(BF16)) for i in ids]
    u_b = [x.astype(BF16) for x in u]
    u_lo = [(u[i] - u_b[i].astype(F32)).astype(BF16) for i in ids]
    l_hi = [l.astype(BF16) for l in l_ab]
    l_lo = [(l_ab[i] - l_hi[i].astype(F32)).astype(BF16) for i in ids]
    res = [rhs[i] - u[i] + (_dot(l_hi[i], u_b[i]) + _dot(l_hi[i], u_lo[i]) + _dot(l_lo[i], u_b[i]))
           for i in ids]
    u = [u[i] + _dot(t_b[i], res[i].astype(BF16)) for i in ids]
    u_b = [x.astype(BF16) for x in u]
    for i in ids:
        g, h = inst[i]
        y_scr[g * chunk:(g + 1) * chunk, h * head_dim:(h + 1) * head_dim] = y_kv[i] + _dot(m_rb[i], u_b[i])
        uv_hi = jnp.concatenate([u_b[i], cut(v_b, g, h)], axis=0)
        uv_lo = jnp.concatenate([(u[i] - u_b[i].astype(F32)).astype(BF16), cut(v_lo, g, h)], axis=0)
        bk_hi = jnp.concatenate([cut(bt_b, g, h), cut(kt_b, g, h)], axis=0)
        bk_lo = jnp.concatenate([cut(bt_lo, g, h), cut(kt_lo, g, h)], axis=0)
        upd = _dot_tn(uv_hi, bk_hi) + _dot_tn(uv_hi, bk_lo) + _dot_tn(uv_lo, bk_hi)
        g_end = g_t[(g + 1) * chunk - 1:(g + 1) * chunk, h * head_dim:(h + 1) * head_dim]
        s_scr[g, h] = (s0[i] + upd) * g_end

    y = y_scr[...]
    inv_n = 1.0 / head_dim
    mean = head_sum(y) * inv_n
    d = y - mean
    var = head_sum(d * d) * inv_n
    yn = d * lax.rsqrt(var + GN_EPS) * lnw_ref[...] + lnb_ref[...]
    bonus = head_sum(r * k_mod * rk_ref[...])
    return (yn + bonus * v) * _silu(z)


def _rwkv(p_rw, prev0, s0, params, bd, *, n_seq, chunk, t_valid, group):
    rows, cols = p_rw.shape
    n_chunks = rows // (n_seq * chunk)
    assert group == 1 or n_chunks == 1
    n_heads, head_dim = s0.shape[1], s0.shape[2]
    width = n_heads * head_dim
    const = lambda arr: pl.BlockSpec(arr.shape, lambda s, c: (0,) * arr.ndim)
    state_spec = pl.BlockSpec((group, n_heads, head_dim, head_dim), lambda s, c: (s, 0, 0, 0))
    return pl.pallas_call(
        functools.partial(_rwkv_kernel, chunk=chunk, t_valid=t_valid,
                          n_heads=n_heads, head_dim=head_dim, group=group),
        grid=(n_seq // group, n_chunks),
        in_specs=[pl.BlockSpec((group * chunk, cols), lambda s, c: (s * n_chunks + c, 0)),
                  pl.BlockSpec((group, 1, cols), lambda s, c: (s, 0, 0)),
                  state_spec]
                 + [const(a) for a in params] + [const(bd)],
        out_specs=[pl.BlockSpec((group * chunk, width), lambda s, c: (s * n_chunks + c, 0)),
                   state_spec],
        out_shape=[jax.ShapeDtypeStruct((rows, width), F32),
                   jax.ShapeDtypeStruct(s0.shape, F32)],
        scratch_shapes=[pltpu.VMEM((group, n_heads, head_dim, head_dim), F32),
                        pltpu.VMEM((group, SUBLANES, cols), F32),
                        pltpu.VMEM((group * chunk, width), F32)],
        compiler_params=pltpu.CompilerParams(
            dimension_semantics=("parallel", "arbitrary"), vmem_limit_bytes=VMEM_LIMIT),
        name="rwkv7_chunk",
    )(p_rw, prev0, s0, *params, bd)


SOFTPLUS2_LINEAR_ABOVE = 100.0


def _softplus2(x):
    return jnp.where(x > SOFTPLUS2_LINEAR_ABOVE, x, jnp.log2(1.0 + jnp.exp2(x)))


def _suffix_matrix(n, with_total):
    m = -np.tril(np.ones((n, n), np.float32), -1)
    return np.concatenate([m, -np.ones((n, LANES), np.float32)], axis=1) if with_total else m


def _sb_prompt_step(qi, kj, bias_ref, q_ref, k_ref, v_ref, m_ref, o_ref,
                    carry_scr, acc_scr, *, blk, n_sub, n_heads, head_dim):
    lanes = carry_scr.shape[-1]
    FULL, DIAG = "full", "diag"

    def sweep(modes):
        kb = k_ref[...].astype(BF16)
        vb = v_ref[...].astype(BF16)
        qb = {s: q_ref[s * blk:(s + 1) * blk, :].astype(BF16) for s in range(n_sub) if modes[s]}
        if DIAG in modes:
            ti = lax.broadcasted_iota(jnp.int32, (blk, blk), 0)
            si = lax.broadcasted_iota(jnp.int32, (blk, blk), 1)
            causal = si < ti
        sl = lambda h: slice(h * head_dim, (h + 1) * head_dim)
        inst = [(s, h) for s in range(n_sub) if modes[s] for h in range(n_heads)]
        zz, log_beta, sums, first = {}, {}, {}, {}
        for i in range(len(inst) + 2):
            if i < len(inst):
                s, h = inst[i]
                zz[i] = _dot_nt(qb[s][:, sl(h)], kb[:, sl(h)]) + bias_ref[h]
            n = i - 1
            if 0 <= n < len(inst):
                diagonal = modes[inst[n][0]] == DIAG
                z_n = zz.pop(n)
                sp = _softplus2(z_n)
                log_beta[n] = z_n - sp
                fail = jnp.where(causal, sp, 0.0) if diagonal else sp
                first[n] = fail[:, 0:1]
                sums[n] = _dot(fail.astype(BF16), m_ref[...])
            n = i - 2
            if 0 <= n < len(inst):
                s, h = inst[n]
                diagonal = modes[s] == DIAG
                rest = sums.pop(n)
                total = jnp.broadcast_to(rest[:, 0:1] - first.pop(n), (blk, lanes))
                wgt = jnp.exp2(log_beta.pop(n) + rest)
                if diagonal:
                    wgt = jnp.where(causal, wgt, 0.0)
                contrib = _dot(wgt.astype(BF16), vb[:, sl(h)])
                if diagonal:
                    carry_scr[s, h] = total
                    acc_scr[s, h] = contrib
                else:
                    carry = carry_scr[s, h]
                    acc_scr[s, h] = acc_scr[s, h] + contrib * jnp.exp2(carry[:, :head_dim])
                    carry_scr[s, h] = carry + total

    for first_live in range(n_sub):
        modes = tuple(None if s < first_live else (DIAG if s == first_live else FULL) for s in range(n_sub))

        @pl.when(kj == qi * n_sub + first_live)
        def _(modes=modes):
            sweep(modes)

    @pl.when(kj < qi * n_sub)
    def _():
        sweep((FULL,) * n_sub)

    @pl.when(kj == 0)
    def _():
        for s in range(n_sub):
            for h in range(n_heads):
                o_ref[s * blk:(s + 1) * blk, h * head_dim:(h + 1) * head_dim] = acc_scr[s, h]


def _sb_decode_step(step, n_steps, bias_ref, q_ref, kn_ref, vn_ref, m_ref, m_new_ref, k_refs, v_refs,
                    o_ref, carry_scr, acc_scr, *, n_heads, head_dim, n_new):
    pages_per_step = len(k_refs)
    n_rows = n_new * n_heads
    width = n_heads * head_dim
    page = m_ref.shape[0]

    row_head = lax.broadcasted_iota(jnp.int32, (n_heads, width), 0)
    lane_head = lax.broadcasted_iota(jnp.int32, (n_heads, width), 1) // head_dim
    head_mask = row_head == lane_head
    q = q_ref[0]
    q_bd = jnp.concatenate(
        [jnp.where(head_mask, jnp.broadcast_to(q[t:t + 1, :], (n_heads, width)), 0.0)
         for t in range(n_new)], axis=0).astype(BF16)
    rh = lax.broadcasted_iota(jnp.int32, (n_rows, 1), 0) % n_heads
    bias_col = jnp.zeros((n_rows, 1), F32)
    for h in range(n_heads):
        bias_col = jnp.where(rh == h, bias_ref[h], bias_col)

    @pl.when(step == 0)
    def _():
        pad = kn_ref.shape[1]
        kb = kn_ref[0].astype(BF16)
        vb = vn_ref[0].astype(BF16)
        zz = _dot_nt(q_bd, kb) + bias_col
        tq = lax.broadcasted_iota(jnp.int32, (n_rows, pad), 0) // n_heads
        tk = lax.broadcasted_iota(jnp.int32, (n_rows, pad), 1)
        causal = tk < tq
        sp = _softplus2(zz)
        fail = jnp.where(causal, sp, 0.0)
        rest_new = _dot(fail, m_new_ref[...], HIGHEST)
        wgt = jnp.where(causal, jnp.exp2(zz - sp + rest_new), 0.0)
        acc_scr[...] = _dot(wgt.astype(BF16), vb)
        carry_scr[...] = jnp.broadcast_to(-jnp.sum(fail, axis=-1, keepdims=True), carry_scr.shape)

    zs = [_dot(q_bd, k_refs[i][0].astype(BF16)) + bias_col for i in range(pages_per_step)]
    sps = [_softplus2(zz) for zz in zs]
    sums = _dot(jnp.concatenate([sp.astype(BF16) for sp in sps], axis=0), m_ref[...])
    acc = acc_scr[...]
    carry = carry_scr[...]
    for i in range(pages_per_step):
        s_i = sums[i * n_rows:(i + 1) * n_rows]
        rest_p = s_i[:, :page] + pltpu.repeat(carry, page // LANES, axis=1)
        wgt = jnp.exp2(zs[i] - sps[i] + rest_p)
        acc = acc + _dot_nt(wgt.astype(BF16), v_refs[i][0].astype(BF16))
        carry = carry + s_i[:, page:]
    acc_scr[...] = acc
    carry_scr[...] = carry

    @pl.when(step == n_steps - 1)
    def _():
        for t in range(n_new):
            blk = jnp.where(head_mask, acc[t * n_heads:(t + 1) * n_heads, :], 0.0)
            o_ref[0, t:t + 1, :] = jnp.sum(blk, axis=0, keepdims=True)


def _sb_kernel(qi_ref, kj_ref, bias_ref, pt_ref, q_ref, k_ref, v_ref, m_ref,
               qd_ref, kn_ref, vn_ref, md_ref, m_new_ref, *rest,
               n_pairs, n_dec, dec_steps, pages_per_step, blk, n_sub, n_heads, head_dim, n_new):
    k_refs = rest[:pages_per_step]
    v_refs = rest[pages_per_step:2 * pages_per_step]
    o_ref, od_ref, carry_scr, acc_scr, carry_d_scr, acc_d_scr = rest[2 * pages_per_step:]
    step = pl.program_id(0)

    @pl.when(step < n_pairs)
    def _():
        _sb_prompt_step(qi_ref[step], kj_ref[step], bias_ref, q_ref, k_ref, v_ref, m_ref, o_ref,
                        carry_scr, acc_scr, blk=blk, n_sub=n_sub, n_heads=n_heads, head_dim=head_dim)

    @pl.when(step < n_dec)
    def _():
        _sb_decode_step(step % dec_steps, dec_steps, bias_ref, qd_ref, kn_ref, vn_ref, md_ref, m_new_ref,
                        k_refs, v_refs, od_ref, carry_d_scr, acc_d_scr,
                        n_heads=n_heads, head_dim=head_dim, n_new=n_new)


def _sb_attention(q, k, v, q_dec, k_new, v_new, cache_kt, cache_vt, page_table, bias,
                  *, blk, n_sub, pages_per_step, n_heads):
    rows, width = q.shape
    head_dim = width // n_heads
    nq = rows // (blk * n_sub)
    pairs = [(i, j) for i in range(nq) for j in range(i * n_sub + n_sub - 1, -1, -1)]
    n_pairs = len(pairs)
    n_seq, n_new, _ = q_dec.shape
    n_pages = page_table.shape[1]
    page = cache_kt.shape[2]
    dec_steps = n_pages // pages_per_step
    n_dec = n_seq * dec_steps
    n_steps = max(n_pairs, n_dec)
    pairs = pairs + [pairs[-1]] * (n_steps - n_pairs)
    qi = jnp.asarray([p[0] for p in pairs], jnp.int32)
    kj = jnp.asarray([p[1] for p in pairs], jnp.int32)
    pad = k_new.shape[1]
    n_rows = n_new * n_heads
    excl = jnp.asarray(_suffix_matrix(blk, False), BF16)
    excl_dec = jnp.asarray(_suffix_matrix(page, True), BF16)
    excl_new = jnp.asarray(-np.tril(np.ones((pad, pad), np.float32), -1), F32)

    def dec_pos(s):
        sd = jnp.minimum(s, n_dec - 1)
        return sd // dec_steps, sd % dec_steps

    def page_spec(i):
        def index_map(s, qi_, kj_, bias_, pt):
            b, ds = dec_pos(s)
            logical = n_pages - 1 - (ds * pages_per_step + i)
            return (pt[b * n_pages + logical], 0, 0)
        return pl.BlockSpec((1, width, page), index_map)

    seq_spec = lambda r: pl.BlockSpec((1, r, width), lambda s, qi_, kj_, bias_, pt: (dec_pos(s)[0], 0, 0))
    const = lambda a: pl.BlockSpec(a.shape, lambda s, qi_, kj_, bias_, pt: (0, 0))
    grid_spec = pltpu.PrefetchScalarGridSpec(
        num_scalar_prefetch=4,
        grid=(n_steps,),
        in_specs=[pl.BlockSpec((blk * n_sub, width), lambda s, qi_, kj_, bias_, pt: (qi_[s], 0)),
                  pl.BlockSpec((blk, width), lambda s, qi_, kj_, bias_, pt: (kj_[s], 0)),
                  pl.BlockSpec((blk, width), lambda s, qi_, kj_, bias_, pt: (kj_[s], 0)),
                  const(excl),
                  seq_spec(n_new), seq_spec(pad), seq_spec(pad), const(excl_dec), const(excl_new)]
                 + [page_spec(i) for i in range(pages_per_step)] * 2,
        out_specs=[pl.BlockSpec((blk * n_sub, width), lambda s, qi_, kj_, bias_, pt: (qi_[s], 0)),
                   seq_spec(n_new)],
        scratch_shapes=[pltpu.VMEM((n_sub, n_heads, blk, LANES), F32),
                        pltpu.VMEM((n_sub, n_heads, blk, head_dim), F32),
                        pltpu.VMEM((n_rows, LANES), F32), pltpu.VMEM((n_rows, width), F32)],
    )
    return pl.pallas_call(
        functools.partial(_sb_kernel, n_pairs=n_pairs, n_dec=n_dec, dec_steps=dec_steps,
                          pages_per_step=pages_per_step, blk=blk, n_sub=n_sub, n_heads=n_heads,
                          head_dim=head_dim, n_new=n_new),
        grid_spec=grid_spec,
        out_shape=[jax.ShapeDtypeStruct((rows, width), F32),
                   jax.ShapeDtypeStruct((n_seq, n_new, width), F32)],
        compiler_params=pltpu.CompilerParams(
            dimension_semantics=("arbitrary",), vmem_limit_bytes=VMEM_LIMIT),
        name="sb_attention",
    )(qi, kj, bias, page_table.reshape(-1), q, k, v, excl, q_dec, k_new, v_new, excl_dec, excl_new,
      *([cache_kt] * pages_per_step), *([cache_vt] * pages_per_step))


def _merge_kernel(x_ref, orw_ref, osb_ref, zsb_ref, grw_ref, gsb_ref, gate_ref, gain_ref,
                  wrw_ref, wsb_ref, wout_ref, y_ref):
    o_sb = osb_ref[...] * _silu(zsb_ref[...])
    m = (_sigmoid(grw_ref[...]) * _dot(orw_ref[...].astype(BF16), wrw_ref[...])
         + _sigmoid(gsb_ref[...]) * _dot(o_sb.astype(BF16), wsb_ref[...]))
    u = _dot(m.astype(BF16), wout_ref[...])
    ms = jnp.mean(u * u, axis=-1, keepdims=True)
    y_ref[...] = x_ref[...] + gate_ref[...] * (u * lax.rsqrt(ms + RMS_EPS) * gain_ref[...])


def _merge(x, o_rw, o_sb, z_sb, g_rw, g_sb, gate, gain, w_rw, w_sb, w_out, tm):
    rows, d = x.shape
    width = o_rw.shape[1]
    per_row = gate.shape[0] != 1
    gate_spec = (pl.BlockSpec((tm, d), lambda i: (i, 0)) if per_row
                 else pl.BlockSpec((1, d), lambda i: (0, 0)))
    row_spec = lambda w: pl.BlockSpec((tm, w), lambda i: (i, 0))
    const = lambda arr: pl.BlockSpec(arr.shape, lambda i: (0, 0))
    return pl.pallas_call(
        _merge_kernel,
        grid=(rows // tm,),
        in_specs=[row_spec(d), row_spec(width), row_spec(width), row_spec(width),
                  row_spec(d), row_spec(d), gate_spec, const(gain),
                  const(w_rw), const(w_sb), const(w_out)],
        out_specs=row_spec(d),
        out_shape=jax.ShapeDtypeStruct((rows, d), F32),
        compiler_params=pltpu.CompilerParams(
            dimension_semantics=("parallel",), vmem_limit_bytes=VMEM_LIMIT),
        name="merge_out",
    )(x, o_rw, o_sb, z_sb, g_rw, g_sb, gate, gain, w_rw, w_sb, w_out)


def _row_tile(rows, want):
    t = min(rows, want)
    while rows % t:
        t //= 2
    return t


def _tiles(seq, rows_sample, n_seq, n_pages):
    sb_blk = _row_tile(seq, 256)
    sb_sub = 2 if seq % (2 * sb_blk) == 0 else 1
    n_q = seq // (sb_blk * sb_sub)
    n_pairs = sum(i * sb_sub + sb_sub for i in range(n_q))
    options = [p for p in (1, 2, 4, 8, 16) if n_pages % p == 0]
    fitting = [p for p in options if n_seq * n_pages // p <= n_pairs]
    return dict(
        proj_rows=_row_tile(seq, 256), proj_rows_sample=_row_tile(rows_sample, 256),
        merge_rows=_row_tile(seq, 512), merge_rows_sample=_row_tile(rows_sample, 512),
        rwkv_chunk=_row_tile(seq, 128), rwkv_group=_row_tile(n_seq, 4),
        sb_blk=sb_blk, sb_sub=sb_sub,
        pages_per_step=fitting[0] if fitting else options[-1])


def kernel(x_prompt, x_sample, cache_sb_k, cache_sb_v, state_rwkv_wkv, state_rwkv_shift, page_table, c_prompt, c_sample, w_ada, b_ada, norm_pre, norm_post, w_in, mu_shift, w0_decay, w_decay_up, a0, w_a_up, k_k, k_a, r_k, ln_x_w, ln_x_b, sb_bias, w_branch_rwkv, w_branch_sb, w_out):
    bp, seq, d = x_prompt.shape
    bs, t_new, _ = x_sample.shape
    rw_heads, rw_dim = r_k.shape
    rw_width = rw_heads * rw_dim
    rw_cols = mu_shift.shape[0]
    n_pool, page, sb_heads, sb_dim = cache_sb_k.shape
    sb_width = sb_heads * sb_dim
    widths = (rw_cols, sb_width, sb_width, sb_width, sb_width, d, d)
    assert sum(widths) == w_in.shape[1]
    q_scale = float(sb_dim) ** -0.5 * LOG2_E
    bias2 = sb_bias * LOG2_E

    n_c = bp + bs
    c_all = jnp.concatenate([c_prompt, c_sample], axis=0)
    c_all = jnp.pad(c_all, ((0, (-n_c) % SUBLANES), (0, 0)))
    mod = _modulation(c_all, w_ada, b_ada)
    shift_all, scale_all, gate_all = mod[:, :d], mod[:, d:2 * d], mod[:, 2 * d:]

    w_in_bf = w_in.astype(BF16)
    w_rw_bf = w_branch_rwkv.astype(BF16)
    w_sb_bf = w_branch_sb.astype(BF16)
    w_out_bf = w_out.astype(BF16)
    gain_pre = norm_pre.reshape(1, d)
    gain_post = norm_post.reshape(1, d)
    row2 = lambda a: a.reshape(1, -1)
    rw_params = (row2(mu_shift), row2(w0_decay), w_decay_up, row2(a0), w_a_up, row2(k_k),
                 row2(k_a), row2(r_k), row2(ln_x_w), row2(ln_x_b))
    head_of = np.arange(rw_width) // rw_dim
    bd = jnp.asarray(head_of[:, None] == head_of[None, :], BF16)

    assert bp == 1, "one prompt sequence: its sweep shares a pallas_call with the paged decode"
    rows_s = bs * t_new
    n_pages = page_table.shape[1]
    tiles = _tiles(seq, rows_s, bs, n_pages)

    x = x_prompt[0]
    p_rw, q, k, v, z, g_rw, g_sb = _project(
        x, scale_all[0:1], shift_all[0:1], gain_pre, w_in_bf, widths, q_scale, tiles["proj_rows"])
    o_rw, wkv_p = _rwkv(p_rw, jnp.zeros((1, 1, rw_cols), F32),
                        jnp.zeros((1, rw_heads, rw_dim, rw_dim), F32), rw_params, bd,
                        n_seq=1, chunk=tiles["rwkv_chunk"], t_valid=tiles["rwkv_chunk"], group=1)

    xs = x_sample.reshape(rows_s, d)
    expand = lambda a: jnp.repeat(a[bp:bp + bs], t_new, axis=0)
    p_rw_s, q_s, k_s, v_s, z_s, g_rw_s, g_sb_s = _project(
        xs, expand(scale_all), expand(shift_all), gain_pre, w_in_bf, widths, q_scale,
        tiles["proj_rows_sample"])
    t_pad = -(-t_new // BF16_ROWS) * BF16_ROWS
    pad_rows = lambda a: jnp.pad(a.reshape(bs, t_new, -1), ((0, 0), (0, t_pad - t_new), (0, 0)))
    o_rw_s, wkv_s = _rwkv(pad_rows(p_rw_s).reshape(bs * t_pad, rw_cols),
                          state_rwkv_shift.reshape(bs, 1, rw_cols), state_rwkv_wkv, rw_params, bd,
                          n_seq=bs, chunk=t_pad, t_valid=t_new, group=tiles["rwkv_group"])
    o_rw_s = o_rw_s.reshape(bs, t_pad, rw_width)[:, :t_new].reshape(rows_s, rw_width)

    to_kt = lambda c: jnp.transpose(c, (0, 2, 3, 1)).reshape(n_pool, sb_width, page)
    o_sb, o_sb_s = _sb_attention(
        q, k, v, q_s.reshape(bs, t_new, sb_width), pad_rows(k_s), pad_rows(v_s),
        to_kt(cache_sb_k), to_kt(cache_sb_v), page_table, bias2,
        blk=tiles["sb_blk"], n_sub=tiles["sb_sub"], pages_per_step=tiles["pages_per_step"],
        n_heads=sb_heads)

    y_p = _merge(x, o_rw, o_sb, z, g_rw, g_sb, gate_all[0:1], gain_post,
                 w_rw_bf, w_sb_bf, w_out_bf, tiles["merge_rows"])
    y_s = _merge(xs, o_rw_s, o_sb_s.reshape(rows_s, sb_width), z_s, g_rw_s, g_sb_s, expand(gate_all),
                 gain_post, w_rw_bf, w_sb_bf, w_out_bf, tiles["merge_rows_sample"])
    return (y_p[None], y_s.reshape(bs, t_new, d),
            k.reshape(bp, seq, sb_heads, sb_dim), v.reshape(bp, seq, sb_heads, sb_dim),
            k_s.reshape(bs, t_new, sb_heads, sb_dim), v_s.reshape(bs, t_new, sb_heads, sb_dim),
            wkv_p, wkv_s, p_rw[seq - 1:seq], p_rw_s.reshape(bs, t_new, rw_cols)[:, -1])
```

```python
import functools

import numpy as np
import jax
import jax.numpy as jnp
from jax import lax
from jax.experimental import pallas as pl
from jax.experimental.pallas import tpu as pltpu

F32 = jnp.float32
BF16 = jnp.bfloat16
HIGHEST = lax.Precision.HIGHEST

RMS_EPS = 1e-6
GN_EPS = 64e-5
NORM_EPS = 1e-12

VMEM_BYTES_V7X = 64 * 1024 * 1024
VMEM_LIMIT = 48 * 1024 * 1024
SUBLANES = 8
LANES = 128
BF16_ROWS = 16
LOG2_E = 1.4426950408889634

NT_DIMS = (((1,), (1,)), ((), ()))
TN_DIMS = (((0,), (0,)), ((), ()))


def _dot(a, b, precision=None):
    return jnp.dot(a, b, preferred_element_type=F32, precision=precision)


def _dot_nt(a, b, precision=None):
    return lax.dot_general(a, b, NT_DIMS, preferred_element_type=F32, precision=precision)


def _dot_tn(a, b, precision=None):
    return lax.dot_general(a, b, TN_DIMS, preferred_element_type=F32, precision=precision)


def _split2(x):
    hi = x.astype(BF16)
    lo = (x - hi.astype(F32)).astype(BF16)
    return hi, lo


def _split3(x):
    h1 = x.astype(BF16)
    r1 = x - h1.astype(F32)
    h2 = r1.astype(BF16)
    h3 = (r1 - h2.astype(F32)).astype(BF16)
    return h1, h2, h3


def _sigmoid(x):
    return 1.0 / (1.0 + jnp.exp(-x))


def _silu(x):
    return x * _sigmoid(x)


def _softplus(x):
    return jnp.maximum(x, 0.0) + jnp.log(1.0 + jnp.exp(-jnp.abs(x)))


def _mod_kernel(c_ref, w_ref, b_ref, o_ref):
    o_ref[...] = _dot(_silu(c_ref[...]), w_ref[...], HIGHEST) + b_ref[...]


def _modulation(c_all, w_ada, b_ada):
    rows, d = c_all.shape
    n_out = w_ada.shape[1]
    tn = d
    return pl.pallas_call(
        _mod_kernel,
        grid=(n_out // tn,),
        in_specs=[pl.BlockSpec((rows, d), lambda j: (0, 0)),
                  pl.BlockSpec((d, tn), lambda j: (0, j)),
                  pl.BlockSpec((1, tn), lambda j: (0, j))],
        out_specs=pl.BlockSpec((rows, tn), lambda j: (0, j)),
        out_shape=jax.ShapeDtypeStruct((rows, n_out), F32),
        compiler_params=pltpu.CompilerParams(vmem_limit_bytes=VMEM_LIMIT),
        name="adaln_mod",
    )(c_all, w_ada, b_ada.reshape(1, n_out))


AUG = 3


def _proj_kernel(x_ref, scale_ref, shift_ref, gain_ref, w_ref, bias_aug_ref,
                 prw_ref, q_ref, k_ref, v_ref, z_ref, grw_ref, gsb_ref, qa_ref, ka_ref, vb_ref,
                 *, cols, q_scale, n_heads):
    x = x_ref[...]
    ms = jnp.mean(x * x, axis=-1, keepdims=True)
    h = x * lax.rsqrt(ms + RMS_EPS) * gain_ref[...]
    h = h * (1.0 + scale_ref[...]) + shift_ref[...]
    hb = h.astype(BF16)
    outs = (prw_ref, q_ref, k_ref, v_ref, z_ref, grw_ref, gsb_ref)
    vals = []
    for idx, ref in enumerate(outs):
        r = _dot(hb, w_ref[:, cols[idx]:cols[idx + 1]])
        if idx == 1:
            r = r * q_scale
        ref[...] = r
        vals.append(r)
    q, k, v = vals[1], vals[2], vals[3]
    tm, width = q.shape
    hd = width // n_heads
    ones = jnp.where(lax.broadcasted_iota(jnp.int32, (tm, hd), 1) < AUG, 1.0, 0.0)
    qa, ka = [], []
    for hh in range(n_heads):
        qa += [q[:, hh * hd:(hh + 1) * hd], ones]
        ka += [k[:, hh * hd:(hh + 1) * hd], jnp.broadcast_to(bias_aug_ref[hh:hh + 1, :], (tm, hd))]
    qa_ref[...] = jnp.concatenate(qa, axis=-1).astype(BF16)
    ka_ref[...] = jnp.concatenate(ka, axis=-1).astype(BF16)
    vb_ref[...] = v.astype(BF16)


def _project(x, scale, shift, gain, w_bf, bias_aug, widths, q_scale, tm):
    rows, d = x.shape
    cols = tuple(int(c) for c in np.cumsum((0,) + tuple(widths)))
    per_row = scale.shape[0] != 1
    mod_spec = (pl.BlockSpec((tm, d), lambda i: (i, 0)) if per_row
                else pl.BlockSpec((1, d), lambda i: (0, 0)))
    n_heads = bias_aug.shape[0]
    width = widths[1]
    assert bias_aug.shape[1] == width // n_heads
    out_widths = tuple(widths) + (2 * width, 2 * width, width)
    out_dtypes = (F32,) * len(widths) + (BF16,) * 3
    return pl.pallas_call(
        functools.partial(_proj_kernel, cols=cols, q_scale=q_scale, n_heads=n_heads),
        grid=(rows // tm,),
        in_specs=[pl.BlockSpec((tm, d), lambda i: (i, 0)), mod_spec, mod_spec,
                  pl.BlockSpec((1, d), lambda i: (0, 0)),
                  pl.BlockSpec(w_bf.shape, lambda i: (0, 0)),
                  pl.BlockSpec(bias_aug.shape, lambda i: (0, 0))],
        out_specs=[pl.BlockSpec((tm, w), lambda i: (i, 0)) for w in out_widths],
        out_shape=[jax.ShapeDtypeStruct((rows, w), t) for w, t in zip(out_widths, out_dtypes)],
        compiler_params=pltpu.CompilerParams(
            dimension_semantics=("parallel",), vmem_limit_bytes=VMEM_LIMIT),
        name="in_proj",
    )(x, scale, shift, gain, w_bf, bias_aug)


def _rwkv_kernel(p_ref, prev0_ref, s0_ref, mu_ref, w0_ref, wdu_ref, a0_ref, wau_ref,
                 kk_ref, ka_ref, rk_ref, lnw_ref, lnb_ref, bd_ref,
                 o_ref, sout_ref, s_scr, prev_scr, y_scr, *, chunk, t_valid, n_heads, head_dim, group):
    c_idx = pl.program_id(1)
    n_chunks = pl.num_programs(1)

    @pl.when(c_idx == 0)
    def _():
        for g in range(group):
            prev_scr[g] = jnp.broadcast_to(prev0_ref[g], prev_scr.shape[1:])
        s_scr[...] = s0_ref[...]

    consts = (mu_ref, w0_ref, wdu_ref, a0_ref, wau_ref, kk_ref, ka_ref, rk_ref, lnw_ref, lnb_ref, bd_ref)
    o_ref[...] = _rwkv_chunk(p_ref[...], consts, s_scr, prev_scr, y_scr, chunk=chunk, t_valid=t_valid,
                             n_heads=n_heads, head_dim=head_dim, group=group)

    @pl.when(c_idx == n_chunks - 1)
    def _():
        sout_ref[...] = s_scr[...]


def _rwkv_chunk(p, consts, s_scr, prev_scr, y_scr, *, chunk, t_valid, n_heads, head_dim, group):
    mu_ref, w0_ref, wdu_ref, a0_ref, wau_ref, kk_ref, ka_ref, rk_ref, lnw_ref, lnb_ref, bd_ref = consts
    width = n_heads * head_dim
    lora = wdu_ref.shape[0]
    n_rows = group * chunk
    row_all = lax.broadcasted_iota(jnp.int32, (n_rows, 1), 0)
    row = row_all % chunk
    prev = pltpu.roll(p, 1, axis=0)
    for g in range(group):
        prev = jnp.where(row_all == g * chunk, prev_scr[g, 0:1, :], prev)
        prev_scr[g] = jnp.broadcast_to(p[(g + 1) * chunk - 1:(g + 1) * chunk, :], prev_scr.shape[1:])
    m = p + (prev - p) * mu_ref[...]
    r = m[:, 0:width]
    k = m[:, width:2 * width]
    v = m[:, 2 * width:3 * width]
    z = m[:, 3 * width:4 * width]
    w_lo = m[:, 4 * width:4 * width + lora]
    a_lo = m[:, 4 * width + lora:4 * width + 2 * lora]

    bd = bd_ref[...]

    def head_sum(x):
        hi, lo = _split2(x)
        return _dot(hi, bd) + _dot(lo, bd)

    w_pre = w0_ref[...] + _dot(jnp.tanh(w_lo), wdu_ref[...], HIGHEST)
    log_w = -jnp.exp(-_softplus(-w_pre) - 0.5)
    a = _sigmoid(a0_ref[...] + _dot(a_lo, wau_ref[...], HIGHEST))
    kk = k * kk_ref[...]
    kk = kk / jnp.maximum(jnp.sqrt(head_sum(kk * kk)), NORM_EPS)
    k_mod = k * (1.0 + (a - 1.0) * ka_ref[...])
    if t_valid < chunk:
        live = row < t_valid
        log_w = jnp.where(live, log_w, 0.0)
        kk = jnp.where(live, kk, 0.0)
        k_mod = jnp.where(live, k_mod, 0.0)
        v = jnp.where(live, v, 0.0)

    ti = lax.broadcasted_iota(jnp.int32, (chunk, chunk), 0)
    si = lax.broadcasted_iota(jnp.int32, (chunk, chunk), 1)
    strict = ti > si
    lower = ti >= si
    eye = jnp.where(ti == si, 1.0, 0.0).astype(F32)
    ta = lax.broadcasted_iota(jnp.int32, (n_rows, n_rows), 0)
    sa = lax.broadcasted_iota(jnp.int32, (n_rows, n_rows), 1)
    tri = jnp.where((ta >= sa) & (ta // chunk == sa // chunk), 1.0, 0.0).astype(BF16)
    l1, l2, l3 = _split3(log_w)
    cum = _dot(tri, l1) + _dot(tri, l2) + _dot(tri, l3)
    g_t = jnp.exp(cum)
    g_inv = jnp.exp(-cum)
    at = -kk * jnp.exp(cum - log_w)
    bt = kk * a * g_inv
    kt = k_mod * g_inv
    rt = r * g_t

    n_double = max(int(np.ceil(np.log2(min(chunk, t_valid)))) - 1, 0)
    at_b, bt_b, kt_b, rt_b, v_b = (x.astype(BF16) for x in (at, bt, kt, rt, v))
    bt_lo = (bt - bt_b.astype(F32)).astype(BF16)
    kt_lo = (kt - kt_b.astype(F32)).astype(BF16)
    v_lo = (v - v_b.astype(F32)).astype(BF16)
    inst = [(g, h) for g in range(group) for h in range(n_heads)]
    ids = range(len(inst))
    cut = lambda x, g, h: x[g * chunk:(g + 1) * chunk, h * head_dim:(h + 1) * head_dim]
    s0 = [s_scr[g, h] for g, h in inst]
    ar = [jnp.concatenate([cut(at_b, g, h), cut(rt_b, g, h)], axis=0) for g, h in inst]
    g_b = [_dot_nt(ar[i], cut(bt_b, *inst[i])) for i in ids]
    g_k = [_dot_nt(ar[i], cut(kt_b, *inst[i])) for i in ids]
    from_state = [_dot_nt(ar[i], s0[i].astype(BF16)) for i in ids]
    l_ab = [jnp.where(strict, x[:chunk], 0.0) for x in g_b]
    l_ak = [jnp.where(strict, x[:chunk], 0.0).astype(BF16) for x in g_k]
    m_rb = [jnp.where(lower, x[chunk:], 0.0).astype(BF16) for x in g_b]
    m_rk = [jnp.where(lower, x[chunk:], 0.0).astype(BF16) for x in g_k]
    rhs = [from_state[i][:chunk] + _dot(l_ak[i], cut(v_b, *inst[i])) for i in ids]
    y_kv = [from_state[i][chunk:] + _dot(m_rk[i], cut(v_b, *inst[i])) for i in ids]
    t_inv = [eye + l for l in l_ab]
    pw_b = [l.astype(BF16) for l in l_ab]
    if n_double:
        pw_b = [_dot(x, x).astype(BF16) for x in pw_b]
    for j in range(n_double):
        t_b = [t.astype(BF16) for t in t_inv]
        t_inv = [t_inv[i] + _dot(t_b[i], pw_b[i]) for i in ids]
        if j + 1 < n_double:
            pw_b = [_dot(x, x).astype(BF16) for x in pw_b]
    t_b = [t.astype(BF16) for t in t_inv]
    u = [_dot(t_b[i], rhs[i].astype(BF16)) for i in ids]
    u_b = [x.astype(BF16) for x in u]
    u_lo = [(u[i] - u_b[i].astype(F32)).astype(BF16) for i in ids]
    l_hi = [l.astype(BF16) for l in l_ab]
    l_lo = [(l_ab[i] - l_hi[i].astype(F32)).astype(BF16) for i in ids]
    res = [rhs[i] - u[i] + (_dot(l_hi[i], u_b[i]) + _dot(l_hi[i], u_lo[i]) + _dot(l_lo[i], u_b[i]))
           for i in ids]
    u = [u[i] + _dot(t_b[i], res[i].astype(BF16)) for i in ids]
    u_b = [x.astype(BF16) for x in u]
    for i in ids:
        g, h = inst[i]
        y_scr[g * chunk:(g + 1) * chunk, h * head_dim:(h + 1) * head_dim] = y_kv[i] + _dot(m_rb[i], u_b[i])
        uv_hi = jnp.concatenate([u_b[i], cut(v_b, g, h)], axis=0)
        uv_lo = jnp.concatenate([(u[i] - u_b[i].astype(F32)).astype(BF16), cut(v_lo, g, h)], axis=0)
        bk_hi = jnp.concatenate([cut(bt_b, g, h), cut(kt_b, g, h)], axis=0)
        bk_lo = jnp.concatenate([cut(bt_lo, g, h), cut(kt_lo, g, h)], axis=0)
        upd = _dot_tn(uv_hi, bk_hi) + _dot_tn(uv_hi, bk_lo) + _dot_tn(uv_lo, bk_hi)
        g_end = g_t[(g + 1) * chunk - 1:(g + 1) * chunk, h * head_dim:(h + 1) * head_dim]
        s_scr[g, h] = (s0[i] + upd) * g_end

    y = y_scr[...]
    inv_n = 1.0 / head_dim
    mean = head_sum(y) * inv_n
    d = y - mean
    var = head_sum(d * d) * inv_n
    yn = d * lax.rsqrt(var + GN_EPS) * lnw_ref[...] + lnb_ref[...]
    bonus = head_sum(r * k_mod * rk_ref[...])
    return (yn + bonus * v) * _silu(z)


def _rwkv(p_rw, prev0, s0, params, bd, *, n_seq, chunk, t_valid, group):
    rows, cols = p_rw.shape
    n_chunks = rows // (n_seq * chunk)
    assert group == 1 or n_chunks == 1
    n_heads, head_dim = s0.shape[1], s0.shape[2]
    width = n_heads * head_dim
    const = lambda arr: pl.BlockSpec(arr.shape, lambda s, c: (0,) * arr.ndim)
    state_spec = pl.BlockSpec((group, n_heads, head_dim, head_dim), lambda s, c: (s, 0, 0, 0))
    return pl.pallas_call(
        functools.partial(_rwkv_kernel, chunk=chunk, t_valid=t_valid,
                          n_heads=n_heads, head_dim=head_dim, group=group),
        grid=(n_seq // group, n_chunks),
        in_specs=[pl.BlockSpec((group * chunk, cols), lambda s, c: (s * n_chunks + c, 0)),
                  pl.BlockSpec((group, 1, cols), lambda s, c: (s, 0, 0)),
                  state_spec]
                 + [const(a) for a in params] + [const(bd)],
        out_specs=[pl.BlockSpec((group * chunk, width), lambda s, c: (s * n_chunks + c, 0)),
                   state_spec],
        out_shape=[jax.ShapeDtypeStruct((rows, width), F32),
                   jax.ShapeDtypeStruct(s0.shape, F32)],
        scratch_shapes=[pltpu.VMEM((group, n_heads, head_dim, head_dim), F32),
                        pltpu.VMEM((group, SUBLANES, cols), F32),
                        pltpu.VMEM((group * chunk, width), F32)],
        compiler_params=pltpu.CompilerParams(
            dimension_semantics=("parallel", "arbitrary"), vmem_limit_bytes=VMEM_LIMIT),
        name="rwkv7_chunk",
    )(p_rw, prev0, s0, *params, bd)


SOFTPLUS2_LINEAR_ABOVE = 100.0


def _softplus2(x):
    return jnp.where(x > SOFTPLUS2_LINEAR_ABOVE, x, jnp.log2(1.0 + jnp.exp2(x)))


def _suffix_matrix(n, with_total):
    m = -np.tril(np.ones((n, n), np.float32), -1)
    return np.concatenate([m, -np.ones((n, LANES), np.float32)], axis=1) if with_total else m


def _sb_prompt_step(qi, kj, bias_ref, q_ref, k_ref, v_ref, m_ref, o_ref,
                    carry_scr, acc_scr, *, blk, n_sub, n_heads, head_dim):
    lanes = carry_scr.shape[-1]
    FULL, DIAG = "full", "diag"

    def sweep(modes):
        aug = lambda h: slice(2 * h * head_dim, 2 * (h + 1) * head_dim)
        if DIAG in modes:
            ti = lax.broadcasted_iota(jnp.int32, (blk, blk), 0)
            si = lax.broadcasted_iota(jnp.int32, (blk, blk), 1)
            causal = si < ti
        sl = lambda h: slice(h * head_dim, (h + 1) * head_dim)
        inst = [(s, h) for s in range(n_sub) if modes[s] for h in range(n_heads)]
        zz, log_beta, sums, first = {}, {}, {}, {}
        for i in range(len(inst) + 2):
            if i < len(inst):
                s, h = inst[i]
                zz[i] = _dot_nt(q_ref[s * blk:(s + 1) * blk, aug(h)], k_ref[:, aug(h)])
            n = i - 1
            if 0 <= n < len(inst):
                diagonal = modes[inst[n][0]] == DIAG
                z_n = zz.pop(n)
                sp = _softplus2(z_n)
                log_beta[n] = z_n - sp
                fail = jnp.where(causal, sp, 0.0) if diagonal else sp
                first[n] = fail[:, 0:1]
                sums[n] = _dot(fail.astype(BF16), m_ref[...])
            n = i - 2
            if 0 <= n < len(inst):
                s, h = inst[n]
                diagonal = modes[s] == DIAG
                rest = sums.pop(n)
                total = jnp.broadcast_to(rest[:, 0:1] - first.pop(n), (blk, lanes))
                wgt = jnp.exp2(log_beta.pop(n) + rest)
                if diagonal:
                    wgt = jnp.where(causal, wgt, 0.0)
                contrib = _dot(wgt.astype(BF16), v_ref[:, sl(h)])
                if diagonal:
                    carry_scr[s, h] = total
                    acc_scr[s, h] = contrib
                else:
                    carry = carry_scr[s, h]
                    acc_scr[s, h] = acc_scr[s, h] + contrib * jnp.exp2(carry[:, :head_dim])
                    carry_scr[s, h] = carry + total

    for first_live in range(n_sub):
        modes = tuple(None if s < first_live else (DIAG if s == first_live else FULL) for s in range(n_sub))

        @pl.when(kj == qi * n_sub + first_live)
        def _(modes=modes):
            sweep(modes)

    @pl.when(kj < qi * n_sub)
    def _():
        sweep((FULL,) * n_sub)

    @pl.when(kj == 0)
    def _():
        for s in range(n_sub):
            for h in range(n_heads):
                o_ref[s * blk:(s + 1) * blk, h * head_dim:(h + 1) * head_dim] = acc_scr[s, h]


def _sb_decode_step(step, n_steps, bias_ref, q_ref, kn_ref, vn_ref, m_ref, m_new_ref, k_refs, v_refs,
                    o_ref, carry_scr, acc_scr, *, n_heads, head_dim, n_new):
    pages_per_step = len(k_refs)
    n_rows = n_new * n_heads
    width = n_heads * head_dim
    page = m_ref.shape[0]

    row_head = lax.broadcasted_iota(jnp.int32, (n_heads, width), 0)
    lane_head = lax.broadcasted_iota(jnp.int32, (n_heads, width), 1) // head_dim
    head_mask = row_head == lane_head
    q = q_ref[0]
    q_bd = jnp.concatenate(
        [jnp.where(head_mask, jnp.broadcast_to(q[t:t + 1, :], (n_heads, width)), 0.0)
         for t in range(n_new)], axis=0).astype(BF16)
    rh = lax.broadcasted_iota(jnp.int32, (n_rows, 1), 0) % n_heads
    bias_col = jnp.zeros((n_rows, 1), F32)
    for h in range(n_heads):
        bias_col = jnp.where(rh == h, bias_ref[h], bias_col)

    @pl.when(step == 0)
    def _():
        pad = kn_ref.shape[1]
        kb = kn_ref[0].astype(BF16)
        vb = vn_ref[0].astype(BF16)
        zz = _dot_nt(q_bd, kb) + bias_col
        tq = lax.broadcasted_iota(jnp.int32, (n_rows, pad), 0) // n_heads
        tk = lax.broadcasted_iota(jnp.int32, (n_rows, pad), 1)
        causal = tk < tq
        sp = _softplus2(zz)
        fail = jnp.where(causal, sp, 0.0)
        rest_new = _dot(fail, m_new_ref[...], HIGHEST)
        wgt = jnp.where(causal, jnp.exp2(zz - sp + rest_new), 0.0)
        acc_scr[...] = _dot(wgt.astype(BF16), vb)
        carry_scr[...] = jnp.broadcast_to(-jnp.sum(fail, axis=-1, keepdims=True), carry_scr.shape)

    zs = [_dot(q_bd, k_refs[i][0].astype(BF16)) + bias_col for i in range(pages_per_step)]
    sps = [_softplus2(zz) for zz in zs]
    sums = _dot(jnp.concatenate([sp.astype(BF16) for sp in sps], axis=0), m_ref[...])
    acc = acc_scr[...]
    carry = carry_scr[...]
    for i in range(pages_per_step):
        s_i = sums[i * n_rows:(i + 1) * n_rows]
        rest_p = s_i[:, :page] + pltpu.repeat(carry, page // LANES, axis=1)
        wgt = jnp.exp2(zs[i] - sps[i] + rest_p)
        acc = acc + _dot_nt(wgt.astype(BF16), v_refs[i][0].astype(BF16))
        carry = carry + s_i[:, page:]
    acc_scr[...] = acc
    carry_scr[...] = carry

    @pl.when(step == n_steps - 1)
    def _():
        for t in range(n_new):
            blk = jnp.where(head_mask, acc[t * n_heads:(t + 1) * n_heads, :], 0.0)
            o_ref[0, t:t + 1, :] = jnp.sum(blk, axis=0, keepdims=True)


def _sb_kernel(qi_ref, kj_ref, bias_ref, pt_ref, q_ref, k_ref, v_ref, m_ref,
               qd_ref, kn_ref, vn_ref, md_ref, m_new_ref, *rest,
               n_pairs, n_dec, dec_steps, pages_per_step, blk, n_sub, n_heads, head_dim, n_new):
    k_refs = rest[:pages_per_step]
    v_refs = rest[pages_per_step:2 * pages_per_step]
    o_ref, od_ref, carry_scr, acc_scr, carry_d_scr, acc_d_scr = rest[2 * pages_per_step:]
    step = pl.program_id(0)

    @pl.when(step < n_pairs)
    def _():
        _sb_prompt_step(qi_ref[step], kj_ref[step], bias_ref, q_ref, k_ref, v_ref, m_ref, o_ref,
                        carry_scr, acc_scr, blk=blk, n_sub=n_sub, n_heads=n_heads, head_dim=head_dim)

    @pl.when(step < n_dec)
    def _():
        _sb_decode_step(step % dec_steps, dec_steps, bias_ref, qd_ref, kn_ref, vn_ref, md_ref, m_new_ref,
                        k_refs, v_refs, od_ref, carry_d_scr, acc_d_scr,
                        n_heads=n_heads, head_dim=head_dim, n_new=n_new)


def _sb_attention(q, k, v, q_dec, k_new, v_new, cache_kt, cache_vt, page_table, bias,
                  *, blk, n_sub, pages_per_step, n_heads):
    rows, width = v.shape
    head_dim = width // n_heads
    nq = rows // (blk * n_sub)
    pairs = [(i, j) for i in range(nq) for j in range(i * n_sub + n_sub - 1, -1, -1)]
    n_pairs = len(pairs)
    n_seq, n_new, _ = q_dec.shape
    n_pages = page_table.shape[1]
    page = cache_kt.shape[2]
    dec_steps = n_pages // pages_per_step
    n_dec = n_seq * dec_steps
    n_steps = max(n_pairs, n_dec)
    pairs = pairs + [pairs[-1]] * (n_steps - n_pairs)
    qi = jnp.asarray([p[0] for p in pairs], jnp.int32)
    kj = jnp.asarray([p[1] for p in pairs], jnp.int32)
    pad = k_new.shape[1]
    n_rows = n_new * n_heads
    excl = jnp.asarray(_suffix_matrix(blk, False), BF16)
    excl_dec = jnp.asarray(_suffix_matrix(page, True), BF16)
    excl_new = jnp.asarray(-np.tril(np.ones((pad, pad), np.float32), -1), F32)

    def dec_pos(s):
        sd = jnp.minimum(s, n_dec - 1)
        return sd // dec_steps, sd % dec_steps

    def page_spec(i):
        def index_map(s, qi_, kj_, bias_, pt):
            b, ds = dec_pos(s)
            logical = n_pages - 1 - (ds * pages_per_step + i)
            return (pt[b * n_pages + logical], 0, 0)
        return pl.BlockSpec((1, width, page), index_map)

    seq_spec = lambda r: pl.BlockSpec((1, r, width), lambda s, qi_, kj_, bias_, pt: (dec_pos(s)[0], 0, 0))
    const = lambda a: pl.BlockSpec(a.shape, lambda s, qi_, kj_, bias_, pt: (0, 0))
    grid_spec = pltpu.PrefetchScalarGridSpec(
        num_scalar_prefetch=4,
        grid=(n_steps,),
        in_specs=[pl.BlockSpec((blk * n_sub, 2 * width), lambda s, qi_, kj_, bias_, pt: (qi_[s], 0)),
                  pl.BlockSpec((blk, 2 * width), lambda s, qi_, kj_, bias_, pt: (kj_[s], 0)),
                  pl.BlockSpec((blk, width), lambda s, qi_, kj_, bias_, pt: (kj_[s], 0)),
                  const(excl),
                  seq_spec(n_new), seq_spec(pad), seq_spec(pad), const(excl_dec), const(excl_new)]
                 + [page_spec(i) for i in range(pages_per_step)] * 2,
        out_specs=[pl.BlockSpec((blk * n_sub, width), lambda s, qi_, kj_, bias_, pt: (qi_[s], 0)),
                   seq_spec(n_new)],
        scratch_shapes=[pltpu.VMEM((n_sub, n_heads, blk, LANES), F32),
                        pltpu.VMEM((n_sub, n_heads, blk, head_dim), F32),
                        pltpu.VMEM((n_rows, LANES), F32), pltpu.VMEM((n_rows, width), F32)],
    )
    return pl.pallas_call(
        functools.partial(_sb_kernel, n_pairs=n_pairs, n_dec=n_dec, dec_steps=dec_steps,
                          pages_per_step=pages_per_step, blk=blk, n_sub=n_sub, n_heads=n_heads,
                          head_dim=head_dim, n_new=n_new),
        grid_spec=grid_spec,
        out_shape=[jax.ShapeDtypeStruct((rows, width), F32),
                   jax.ShapeDtypeStruct((n_seq, n_new, width), F32)],
        compiler_params=pltpu.CompilerParams(
            dimension_semantics=("arbitrary",), vmem_limit_bytes=VMEM_LIMIT),
        name="sb_attention",
    )(qi, kj, bias, page_table.reshape(-1), q, k, v, excl, q_dec, k_new, v_new, excl_dec, excl_new,
      *([cache_kt] * pages_per_step), *([cache_vt] * pages_per_step))


def _merge_kernel(x_ref, orw_ref, osb_ref, zsb_ref, grw_ref, gsb_ref, gate_ref, gain_ref,
                  wrw_ref, wsb_ref, wout_ref, y_ref):
    o_sb = osb_ref[...] * _silu(zsb_ref[...])
    m = (_sigmoid(grw_ref[...]) * _dot(orw_ref[...].astype(BF16), wrw_ref[...])
         + _sigmoid(gsb_ref[...]) * _dot(o_sb.astype(BF16), wsb_ref[...]))
    u = _dot(m.astype(BF16), wout_ref[...])
    ms = jnp.mean(u * u, axis=-1, keepdims=True)
    y_ref[...] = x_ref[...] + gate_ref[...] * (u * lax.rsqrt(ms + RMS_EPS) * gain_ref[...])


def _merge(x, o_rw, o_sb, z_sb, g_rw, g_sb, gate, gain, w_rw, w_sb, w_out, tm):
    rows, d = x.shape
    width = o_rw.shape[1]
    per_row = gate.shape[0] != 1
    gate_spec = (pl.BlockSpec((tm, d), lambda i: (i, 0)) if per_row
                 else pl.BlockSpec((1, d), lambda i: (0, 0)))
    row_spec = lambda w: pl.BlockSpec((tm, w), lambda i: (i, 0))
    const = lambda arr: pl.BlockSpec(arr.shape, lambda i: (0, 0))
    return pl.pallas_call(
        _merge_kernel,
        grid=(rows // tm,),
        in_specs=[row_spec(d), row_spec(width), row_spec(width), row_spec(width),
                  row_spec(d), row_spec(d), gate_spec, const(gain),
                  const(w_rw), const(w_sb), const(w_out)],
        out_specs=row_spec(d),
        out_shape=jax.ShapeDtypeStruct((rows, d), F32),
        compiler_params=pltpu.CompilerParams(
            dimension_semantics=("parallel",), vmem_limit_bytes=VMEM_LIMIT),
        name="merge_out",
    )(x, o_rw, o_sb, z_sb, g_rw, g_sb, gate, gain, w_rw, w_sb, w_out)


def _row_tile(rows, want):
    t = min(rows, want)
    while rows % t:
        t //= 2
    return t


def _tiles(seq, rows_sample, n_seq, n_pages):
    sb_blk = _row_tile(seq, 256)
    sb_sub = 2 if seq % (2 * sb_blk) == 0 else 1
    n_q = seq // (sb_blk * sb_sub)
    n_pairs = sum(i * sb_sub + sb_sub for i in range(n_q))
    options = [p for p in (1, 2, 4, 8, 16) if n_pages % p == 0]
    fitting = [p for p in options if n_seq * n_pages // p <= n_pairs]
    return dict(
        proj_rows=_row_tile(seq, 256), proj_rows_sample=_row_tile(rows_sample, 256),
        merge_rows=_row_tile(seq, 512), merge_rows_sample=_row_tile(rows_sample, 512),
        rwkv_chunk=_row_tile(seq, 128), rwkv_group=_row_tile(n_seq, 4),
        sb_blk=sb_blk, sb_sub=sb_sub,
        pages_per_step=fitting[0] if fitting else options[-1])


def kernel(x_prompt, x_sample, cache_sb_k, cache_sb_v, state_rwkv_wkv, state_rwkv_shift, page_table, c_prompt, c_sample, w_ada, b_ada, norm_pre, norm_post, w_in, mu_shift, w0_decay, w_decay_up, a0, w_a_up, k_k, k_a, r_k, ln_x_w, ln_x_b, sb_bias, w_branch_rwkv, w_branch_sb, w_out):
    bp, seq, d = x_prompt.shape
    bs, t_new, _ = x_sample.shape
    rw_heads, rw_dim = r_k.shape
    rw_width = rw_heads * rw_dim
    rw_cols = mu_shift.shape[0]
    n_pool, page, sb_heads, sb_dim = cache_sb_k.shape
    sb_width = sb_heads * sb_dim
    widths = (rw_cols, sb_width, sb_width, sb_width, sb_width, d, d)
    assert sum(widths) == w_in.shape[1]
    q_scale = float(sb_dim) ** -0.5 * LOG2_E
    bias2 = sb_bias * LOG2_E
    pieces, left = [], bias2
    for _ in range(AUG):
        pieces.append(left.astype(BF16).astype(F32))
        left = left - pieces[-1]
    bias_aug = jnp.pad(jnp.stack(pieces, axis=1), ((0, 0), (0, sb_dim - AUG)))

    n_c = bp + bs
    c_all = jnp.concatenate([c_prompt, c_sample], axis=0)
    c_all = jnp.pad(c_all, ((0, (-n_c) % SUBLANES), (0, 0)))
    mod = _modulation(c_all, w_ada, b_ada)
    shift_all, scale_all, gate_all = mod[:, :d], mod[:, d:2 * d], mod[:, 2 * d:]

    w_in_bf = w_in.astype(BF16)
    w_rw_bf = w_branch_rwkv.astype(BF16)
    w_sb_bf = w_branch_sb.astype(BF16)
    w_out_bf = w_out.astype(BF16)
    gain_pre = norm_pre.reshape(1, d)
    gain_post = norm_post.reshape(1, d)
    row2 = lambda a: a.reshape(1, -1)
    rw_params = (row2(mu_shift), row2(w0_decay), w_decay_up, row2(a0), w_a_up, row2(k_k),
                 row2(k_a), row2(r_k), row2(ln_x_w), row2(ln_x_b))
    head_of = np.arange(rw_width) // rw_dim
    bd = jnp.asarray(head_of[:, None] == head_of[None, :], BF16)

    assert bp == 1, "one prompt sequence: its sweep shares a pallas_call with the paged decode"
    rows_s = bs * t_new
    n_pages = page_table.shape[1]
    tiles = _tiles(seq, rows_s, bs, n_pages)

    x = x_prompt[0]
    p_rw, _, k, v, z, g_rw, g_sb, q_aug, k_aug, v_bf = _project(
        x, scale_all[0:1], shift_all[0:1], gain_pre, w_in_bf, bias_aug, widths, q_scale,
        tiles["proj_rows"])
    o_rw, wkv_p = _rwkv(p_rw, jnp.zeros((1, 1, rw_cols), F32),
                        jnp.zeros((1, rw_heads, rw_dim, rw_dim), F32), rw_params, bd,
                        n_seq=1, chunk=tiles["rwkv_chunk"], t_valid=tiles["rwkv_chunk"], group=1)

    xs = x_sample.reshape(rows_s, d)
    expand = lambda a: jnp.repeat(a[bp:bp + bs], t_new, axis=0)
    p_rw_s, q_s, k_s, v_s, z_s, g_rw_s, g_sb_s, _, _, _ = _project(
        xs, expand(scale_all), expand(shift_all), gain_pre, w_in_bf, bias_aug, widths, q_scale,
        tiles["proj_rows_sample"])
    t_pad = -(-t_new // BF16_ROWS) * BF16_ROWS
    pad_rows = lambda a: jnp.pad(a.reshape(bs, t_new, -1), ((0, 0), (0, t_pad - t_new), (0, 0)))
    o_rw_s, wkv_s = _rwkv(pad_rows(p_rw_s).reshape(bs * t_pad, rw_cols),
                          state_rwkv_shift.reshape(bs, 1, rw_cols), state_rwkv_wkv, rw_params, bd,
                          n_seq=bs, chunk=t_pad, t_valid=t_new, group=tiles["rwkv_group"])
    o_rw_s = o_rw_s.reshape(bs, t_pad, rw_width)[:, :t_new].reshape(rows_s, rw_width)

    to_kt = lambda c: jnp.transpose(c, (0, 2, 3, 1)).reshape(n_pool, sb_width, page)
    o_sb, o_sb_s = _sb_attention(
        q_aug, k_aug, v_bf, q_s.reshape(bs, t_new, sb_width), pad_rows(k_s), pad_rows(v_s),
        to_kt(cache_sb_k), to_kt(cache_sb_v), page_table, bias2,
        blk=tiles["sb_blk"], n_sub=tiles["sb_sub"], pages_per_step=tiles["pages_per_step"],
        n_heads=sb_heads)

    y_p = _merge(x, o_rw, o_sb, z, g_rw, g_sb, gate_all[0:1], gain_post,
                 w_rw_bf, w_sb_bf, w_out_bf, tiles["merge_rows"])
    y_s = _merge(xs, o_rw_s, o_sb_s.reshape(rows_s, sb_width), z_s, g_rw_s, g_sb_s, expand(gate_all),
                 gain_post, w_rw_bf, w_sb_bf, w_out_bf, tiles["merge_rows_sample"])
    return (y_p[None], y_s.reshape(bs, t_new, d),
            k.reshape(bp, seq, sb_heads, sb_dim), v.reshape(bp, seq, sb_heads, sb_dim),
            k_s.reshape(bs, t_new, sb_heads, sb_dim), v_s.reshape(bs, t_new, sb_heads, sb_dim),
            wkv_p, wkv_s, p_rw[seq - 1:seq], p_rw_s.reshape(bs, t_new, rw_cols)[:, -1])
```

```python
import functools

import numpy as np
import jax
import jax.numpy as jnp
from jax import lax
from jax.experimental import pallas as pl
from jax.experimental.pallas import tpu as pltpu

F32 = jnp.float32
BF16 = jnp.bfloat16
HIGHEST = lax.Precision.HIGHEST

RMS_EPS = 1e-6
GN_EPS = 64e-5
NORM_EPS = 1e-12

VMEM_BYTES_V7X = 64 * 1024 * 1024
VMEM_LIMIT = 48 * 1024 * 1024
SUBLANES = 8
LANES = 128
BF16_ROWS = 16
LOG2_E = 1.4426950408889634

NT_DIMS = (((1,), (1,)), ((), ()))
TN_DIMS = (((0,), (0,)), ((), ()))


def _dot(a, b, precision=None):
    return jnp.dot(a, b, preferred_element_type=F32, precision=precision)


def _dot_nt(a, b, precision=None):
    return lax.dot_general(a, b, NT_DIMS, preferred_element_type=F32, precision=precision)


def _dot_tn(a, b, precision=None):
    return lax.dot_general(a, b, TN_DIMS, preferred_element_type=F32, precision=precision)


def _split2(x):
    hi = x.astype(BF16)
    lo = (x - hi.astype(F32)).astype(BF16)
    return hi, lo


def _split3(x):
    h1 = x.astype(BF16)
    r1 = x - h1.astype(F32)
    h2 = r1.astype(BF16)
    h3 = (r1 - h2.astype(F32)).astype(BF16)
    return h1, h2, h3


def _sigmoid(x):
    return 1.0 / (1.0 + jnp.exp(-x))


def _silu(x):
    return x * _sigmoid(x)


def _softplus(x):
    return jnp.maximum(x, 0.0) + jnp.log(1.0 + jnp.exp(-jnp.abs(x)))


def _mod_kernel(c_ref, w_ref, b_ref, o_ref):
    o_ref[...] = _dot(_silu(c_ref[...]), w_ref[...], HIGHEST) + b_ref[...]


def _modulation(c_all, w_ada, b_ada):
    rows, d = c_all.shape
    n_out = w_ada.shape[1]
    tn = d
    return pl.pallas_call(
        _mod_kernel,
        grid=(n_out // tn,),
        in_specs=[pl.BlockSpec((rows, d), lambda j: (0, 0)),
                  pl.BlockSpec((d, tn), lambda j: (0, j)),
                  pl.BlockSpec((1, tn), lambda j: (0, j))],
        out_specs=pl.BlockSpec((rows, tn), lambda j: (0, j)),
        out_shape=jax.ShapeDtypeStruct((rows, n_out), F32),
        compiler_params=pltpu.CompilerParams(vmem_limit_bytes=VMEM_LIMIT),
        name="adaln_mod",
    )(c_all, w_ada, b_ada.reshape(1, n_out))


AUG = 3


def _proj_kernel(x_ref, scale_ref, shift_ref, gain_ref, w_ref, bias_aug_ref,
                 prw_ref, q_ref, k_ref, v_ref, z_ref, grw_ref, gsb_ref, qa_ref, ka_ref, vb_ref,
                 *, cols, q_scale, n_heads):
    x = x_ref[...]
    ms = jnp.mean(x * x, axis=-1, keepdims=True)
    h = x * lax.rsqrt(ms + RMS_EPS) * gain_ref[...]
    h = h * (1.0 + scale_ref[...]) + shift_ref[...]
    hb = h.astype(BF16)
    outs = (prw_ref, q_ref, k_ref, v_ref, z_ref, grw_ref, gsb_ref)
    vals = []
    for idx, ref in enumerate(outs):
        r = _dot(hb, w_ref[:, cols[idx]:cols[idx + 1]])
        if idx == 1:
            r = r * q_scale
        ref[...] = r
        vals.append(r)
    q, k, v = vals[1], vals[2], vals[3]
    tm, width = q.shape
    hd = width // n_heads
    ones = jnp.where(lax.broadcasted_iota(jnp.int32, (tm, hd), 1) < AUG, 1.0, 0.0)
    qa, ka = [], []
    for hh in range(n_heads):
        qa += [q[:, hh * hd:(hh + 1) * hd], ones]
        ka += [k[:, hh * hd:(hh + 1) * hd], jnp.broadcast_to(bias_aug_ref[hh:hh + 1, :], (tm, hd))]
    qa_ref[...] = jnp.concatenate(qa, axis=-1).astype(BF16)
    ka_ref[...] = jnp.concatenate(ka, axis=-1).astype(BF16)
    vb_ref[...] = v.astype(BF16)


def _project(x, scale, shift, gain, w_bf, bias_aug, widths, q_scale, tm):
    rows, d = x.shape
    cols = tuple(int(c) for c in np.cumsum((0,) + tuple(widths)))
    per_row = scale.shape[0] != 1
    mod_spec = (pl.BlockSpec((tm, d), lambda i: (i, 0)) if per_row
                else pl.BlockSpec((1, d), lambda i: (0, 0)))
    n_heads = bias_aug.shape[0]
    width = widths[1]
    assert bias_aug.shape[1] == width // n_heads
    out_widths = tuple(widths) + (2 * width, 2 * width, width)
    out_dtypes = (F32,) * len(widths) + (BF16,) * 3
    return pl.pallas_call(
        functools.partial(_proj_kernel, cols=cols, q_scale=q_scale, n_heads=n_heads),
        grid=(rows // tm,),
        in_specs=[pl.BlockSpec((tm, d), lambda i: (i, 0)), mod_spec, mod_spec,
                  pl.BlockSpec((1, d), lambda i: (0, 0)),
                  pl.BlockSpec(w_bf.shape, lambda i: (0, 0)),
                  pl.BlockSpec(bias_aug.shape, lambda i: (0, 0))],
        out_specs=[pl.BlockSpec((tm, w), lambda i: (i, 0)) for w in out_widths],
        out_shape=[jax.ShapeDtypeStruct((rows, w), t) for w, t in zip(out_widths, out_dtypes)],
        compiler_params=pltpu.CompilerParams(
            dimension_semantics=("parallel",), vmem_limit_bytes=VMEM_LIMIT),
        name="in_proj",
    )(x, scale, shift, gain, w_bf, bias_aug)


def _rwkv_kernel(p_ref, prev0_ref, s0_ref, mu_ref, w0_ref, wdu_ref, a0_ref, wau_ref,
                 kk_ref, ka_ref, rk_ref, lnw_ref, lnb_ref, bd_ref,
                 o_ref, sout_ref, s_scr, prev_scr, y_scr, *, chunk, t_valid, n_heads, head_dim, group):
    c_idx = pl.program_id(1)
    n_chunks = pl.num_programs(1)

    @pl.when(c_idx == 0)
    def _():
        for g in range(group):
            prev_scr[g] = jnp.broadcast_to(prev0_ref[g], prev_scr.shape[1:])
        s_scr[...] = s0_ref[...]

    consts = (mu_ref, w0_ref, wdu_ref, a0_ref, wau_ref, kk_ref, ka_ref, rk_ref, lnw_ref, lnb_ref, bd_ref)
    o_ref[...] = _rwkv_chunk(p_ref[...], consts, s_scr, prev_scr, y_scr, chunk=chunk, t_valid=t_valid,
                             n_heads=n_heads, head_dim=head_dim, group=group)

    @pl.when(c_idx == n_chunks - 1)
    def _():
        sout_ref[...] = s_scr[...]


def _rwkv_chunk(p, consts, s_scr, prev_scr, y_scr, *, chunk, t_valid, n_heads, head_dim, group):
    mu_ref, w0_ref, wdu_ref, a0_ref, wau_ref, kk_ref, ka_ref, rk_ref, lnw_ref, lnb_ref, bd_ref = consts
    width = n_heads * head_dim
    lora = wdu_ref.shape[0]
    n_rows = group * chunk
    row_all = lax.broadcasted_iota(jnp.int32, (n_rows, 1), 0)
    row = row_all % chunk
    prev = pltpu.roll(p, 1, axis=0)
    for g in range(group):
        prev = jnp.where(row_all == g * chunk, prev_scr[g, 0:1, :], prev)
        prev_scr[g] = jnp.broadcast_to(p[(g + 1) * chunk - 1:(g + 1) * chunk, :], prev_scr.shape[1:])
    m = p + (prev - p) * mu_ref[...]
    r = m[:, 0:width]
    k = m[:, width:2 * width]
    v = m[:, 2 * width:3 * width]
    z = m[:, 3 * width:4 * width]
    w_lo = m[:, 4 * width:4 * width + lora]
    a_lo = m[:, 4 * width + lora:4 * width + 2 * lora]

    bd = bd_ref[...]

    def head_sum(x):
        hi, lo = _split2(x)
        return _dot(hi, bd) + _dot(lo, bd)

    w_pre = w0_ref[...] + _dot(jnp.tanh(w_lo), wdu_ref[...], HIGHEST)
    log_w = -jnp.exp(-_softplus(-w_pre) - 0.5)
    a = _sigmoid(a0_ref[...] + _dot(a_lo, wau_ref[...], HIGHEST))
    kk = k * kk_ref[...]
    kk = kk / jnp.maximum(jnp.sqrt(head_sum(kk * kk)), NORM_EPS)
    k_mod = k * (1.0 + (a - 1.0) * ka_ref[...])
    if t_valid < chunk:
        live = row < t_valid
        log_w = jnp.where(live, log_w, 0.0)
        kk = jnp.where(live, kk, 0.0)
        k_mod = jnp.where(live, k_mod, 0.0)
        v = jnp.where(live, v, 0.0)

    ti = lax.broadcasted_iota(jnp.int32, (chunk, chunk), 0)
    si = lax.broadcasted_iota(jnp.int32, (chunk, chunk), 1)
    strict = ti > si
    lower = ti >= si
    eye = jnp.where(ti == si, 1.0, 0.0).astype(F32)
    ta = lax.broadcasted_iota(jnp.int32, (n_rows, n_rows), 0)
    sa = lax.broadcasted_iota(jnp.int32, (n_rows, n_rows), 1)
    tri = jnp.where((ta >= sa) & (ta // chunk == sa // chunk), 1.0, 0.0).astype(BF16)
    l1, l2, l3 = _split3(log_w)
    cum = _dot(tri, l1) + _dot(tri, l2) + _dot(tri, l3)
    g_t = jnp.exp(cum)
    g_inv = jnp.exp(-cum)
    at = -kk * jnp.exp(cum - log_w)
    bt = kk * a * g_inv
    kt = k_mod * g_inv
    rt = r * g_t

    n_double = max(int(np.ceil(np.log2(min(chunk, t_valid)))) - 1, 0)
    at_b, bt_b, kt_b, rt_b, v_b = (x.astype(BF16) for x in (at, bt, kt, rt, v))
    bt_lo = (bt - bt_b.astype(F32)).astype(BF16)
    kt_lo = (kt - kt_b.astype(F32)).astype(BF16)
    v_lo = (v - v_b.astype(F32)).astype(BF16)
    inst = [(g, h) for g in range(group) for h in range(n_heads)]
    ids = range(len(inst))
    cut = lambda x, g, h: x[g * chunk:(g + 1) * chunk, h * head_dim:(h + 1) * head_dim]
    s0 = [s_scr[g, h] for g, h in inst]
    ar = [jnp.concatenate([cut(at_b, g, h), cut(rt_b, g, h)], axis=0) for g, h in inst]
    g_b = [_dot_nt(ar[i], cut(bt_b, *inst[i])) for i in ids]
    g_k = [_dot_nt(ar[i], cut(kt_b, *inst[i])) for i in ids]
    from_state = [_dot_nt(ar[i], s0[i].astype(BF16)) for i in ids]
    l_ab = [jnp.where(strict, x[:chunk], 0.0) for x in g_b]
    l_ak = [jnp.where(strict, x[:chunk], 0.0).astype(BF16) for x in g_k]
    m_rb = [jnp.where(lower, x[chunk:], 0.0).astype(BF16) for x in g_b]
    m_rk = [jnp.where(lower, x[chunk:], 0.0).astype(BF16) for x in g_k]
    rhs = [from_state[i][:chunk] + _dot(l_ak[i], cut(v_b, *inst[i])) for i in ids]
    y_kv = [from_state[i][chunk:] + _dot(m_rk[i], cut(v_b, *inst[i])) for i in ids]
    t_inv = [eye + l for l in l_ab]
    pw_b = [l.astype(BF16) for l in l_ab]
    if n_double:
        pw_b = [_dot(x, x).astype(BF16) for x in pw_b]
    for j in range(n_double):
        t_b = [t.astype(BF16) for t in t_inv]
        t_inv = [t_inv[i] + _dot(t_b[i], pw_b[i]) for i in ids]
        if j + 1 < n_double:
            pw_b = [_dot(x, x).astype(BF16) for x in pw_b]
    t_b = [t.astype(BF16) for t in t_inv]
    u = [_dot(t_b[i], rhs[i].astype(BF16)) for i in ids]
    u_b = [x.astype(BF16) for x in u]
    u_lo = [(u[i] - u_b[i].astype(F32)).astype(BF16) for i in ids]
    l_hi = [l.astype(BF16) for l in l_ab]
    l_lo = [(l_ab[i] - l_hi[i].astype(F32)).astype(BF16) for i in ids]
    res = [rhs[i] - u[i] + (_dot(l_hi[i], u_b[i]) + _dot(l_hi[i], u_lo[i]) + _dot(l_lo[i], u_b[i]))
           for i in ids]
    u = [u[i] + _dot(t_b[i], res[i].astype(BF16)) for i in ids]
    u_b = [x.astype(BF16) for x in u]
    for i in ids:
        g, h = inst[i]
        y_scr[g * chunk:(g + 1) * chunk, h * head_dim:(h + 1) * head_dim] = y_kv[i] + _dot(m_rb[i], u_b[i])
        uv_hi = jnp.concatenate([u_b[i], cut(v_b, g, h)], axis=0)
        uv_lo = jnp.concatenate([(u[i] - u_b[i].astype(F32)).astype(BF16), cut(v_lo, g, h)], axis=0)
        bk_hi = jnp.concatenate([cut(bt_b, g, h), cut(kt_b, g, h)], axis=0)
        bk_lo = jnp.concatenate([cut(bt_lo, g, h), cut(kt_lo, g, h)], axis=0)
        upd = _dot_tn(uv_hi, bk_hi) + _dot_tn(uv_hi, bk_lo) + _dot_tn(uv_lo, bk_hi)
        g_end = g_t[(g + 1) * chunk - 1:(g + 1) * chunk, h * head_dim:(h + 1) * head_dim]
        s_scr[g, h] = (s0[i] + upd) * g_end

    y = y_scr[...]
    inv_n = 1.0 / head_dim
    mean = head_sum(y) * inv_n
    d = y - mean
    var = head_sum(d * d) * inv_n
    yn = d * lax.rsqrt(var + GN_EPS) * lnw_ref[...] + lnb_ref[...]
    bonus = head_sum(r * k_mod * rk_ref[...])
    return (yn + bonus * v) * _silu(z)


def _rwkv(p_rw, prev0, s0, params, bd, *, n_seq, chunk, t_valid, group):
    rows, cols = p_rw.shape
    n_chunks = rows // (n_seq * chunk)
    assert group == 1 or n_chunks == 1
    n_heads, head_dim = s0.shape[1], s0.shape[2]
    width = n_heads * head_dim
    const = lambda arr: pl.BlockSpec(arr.shape, lambda s, c: (0,) * arr.ndim)
    state_spec = pl.BlockSpec((group, n_heads, head_dim, head_dim), lambda s, c: (s, 0, 0, 0))
    return pl.pallas_call(
        functools.partial(_rwkv_kernel, chunk=chunk, t_valid=t_valid,
                          n_heads=n_heads, head_dim=head_dim, group=group),
        grid=(n_seq // group, n_chunks),
        in_specs=[pl.BlockSpec((group * chunk, cols), lambda s, c: (s * n_chunks + c, 0)),
                  pl.BlockSpec((group, 1, cols), lambda s, c: (s, 0, 0)),
                  state_spec]
                 + [const(a) for a in params] + [const(bd)],
        out_specs=[pl.BlockSpec((group * chunk, width), lambda s, c: (s * n_chunks + c, 0)),
                   state_spec],
        out_shape=[jax.ShapeDtypeStruct((rows, width), F32),
                   jax.ShapeDtypeStruct(s0.shape, F32)],
        scratch_shapes=[pltpu.VMEM((group, n_heads, head_dim, head_dim), F32),
                        pltpu.VMEM((group, SUBLANES, cols), F32),
                        pltpu.VMEM((group * chunk, width), F32)],
        compiler_params=pltpu.CompilerParams(
            dimension_semantics=("parallel", "arbitrary"), vmem_limit_bytes=VMEM_LIMIT),
        name="rwkv7_chunk",
    )(p_rw, prev0, s0, *params, bd)


SOFTPLUS2_LINEAR_ABOVE = 100.0


def _softplus2(x):
    return jnp.where(x > SOFTPLUS2_LINEAR_ABOVE, x, jnp.log2(1.0 + jnp.exp2(x)))


def _suffix_matrix(n, with_total):
    m = -np.tril(np.ones((n, n), np.float32), -1)
    return np.concatenate([m, -np.ones((n, LANES), np.float32)], axis=1) if with_total else m


def _sb_prompt_step(qi, kj, bias_ref, q_ref, k_ref, v_ref, m_ref, o_ref,
                    carry_scr, acc_scr, *, blk, n_sub, n_heads, head_dim):
    lanes = carry_scr.shape[-1]
    FULL, DIAG = "full", "diag"

    def sweep(modes):
        aug = lambda h: slice(2 * h * head_dim, 2 * (h + 1) * head_dim)
        if DIAG in modes:
            ti = lax.broadcasted_iota(jnp.int32, (blk, blk), 0)
            si = lax.broadcasted_iota(jnp.int32, (blk, blk), 1)
            causal = si < ti
        sl = lambda h: slice(h * head_dim, (h + 1) * head_dim)
        inst = [(s, h) for s in range(n_sub) if modes[s] for h in range(n_heads)]
        zz, log_beta, sums, first = {}, {}, {}, {}
        for i in range(len(inst) + 2):
            if i < len(inst):
                s, h = inst[i]
                zz[i] = _dot_nt(q_ref[s * blk:(s + 1) * blk, aug(h)], k_ref[:, aug(h)])
            n = i - 1
            if 0 <= n < len(inst):
                diagonal = modes[inst[n][0]] == DIAG
                z_n = zz.pop(n)
                sp = _softplus2(z_n)
                log_beta[n] = z_n - sp
                fail = jnp.where(causal, sp, 0.0) if diagonal else sp
                fail_b = fail.astype(BF16)
                sums[n] = _dot(fail_b, m_ref[...])
                first[n] = _dot_nt(jnp.full((BF16_ROWS, blk), -1.0, BF16), fail_b)
            n = i - 2
            if 0 <= n < len(inst):
                s, h = inst[n]
                diagonal = modes[s] == DIAG
                total = first.pop(n)
                wgt = jnp.exp2(log_beta.pop(n) + sums.pop(n))
                if diagonal:
                    wgt = jnp.where(causal, wgt, 0.0)
                contrib = _dot_nt(v_ref[sl(h), :], wgt.astype(BF16))
                if diagonal:
                    carry_scr[s, h] = total
                    acc_scr[s, h] = contrib
                else:
                    carry = carry_scr[s, h]
                    acc_scr[s, h] = acc_scr[s, h] + contrib * jnp.exp2(carry[0:1, :])
                    carry_scr[s, h] = carry + total

    for first_live in range(n_sub):
        modes = tuple(None if s < first_live else (DIAG if s == first_live else FULL) for s in range(n_sub))

        @pl.when(kj == qi * n_sub + first_live)
        def _(modes=modes):
            sweep(modes)

    @pl.when(kj < qi * n_sub)
    def _():
        sweep((FULL,) * n_sub)

    @pl.when(kj == 0)
    def _():
        for s in range(n_sub):
            for h in range(n_heads):
                o_ref[s * blk:(s + 1) * blk, h * head_dim:(h + 1) * head_dim] = acc_scr[s, h].T


def _sb_decode_step(step, n_steps, bias_ref, q_ref, kn_ref, vn_ref, m_ref, m_new_ref, k_refs, v_refs,
                    o_ref, carry_scr, acc_scr, *, n_heads, head_dim, n_new):
    pages_per_step = len(k_refs)
    n_rows = n_new * n_heads
    width = n_heads * head_dim
    page = m_ref.shape[0]

    row_head = lax.broadcasted_iota(jnp.int32, (n_heads, width), 0)
    lane_head = lax.broadcasted_iota(jnp.int32, (n_heads, width), 1) // head_dim
    head_mask = row_head == lane_head
    q = q_ref[0]
    q_bd = jnp.concatenate(
        [jnp.where(head_mask, jnp.broadcast_to(q[t:t + 1, :], (n_heads, width)), 0.0)
         for t in range(n_new)], axis=0).astype(BF16)
    rh = lax.broadcasted_iota(jnp.int32, (n_rows, 1), 0) % n_heads
    bias_col = jnp.zeros((n_rows, 1), F32)
    for h in range(n_heads):
        bias_col = jnp.where(rh == h, bias_ref[h], bias_col)

    @pl.when(step == 0)
    def _():
        pad = kn_ref.shape[1]
        kb = kn_ref[0].astype(BF16)
        vb = vn_ref[0].astype(BF16)
        zz = _dot_nt(q_bd, kb) + bias_col
        tq = lax.broadcasted_iota(jnp.int32, (n_rows, pad), 0) // n_heads
        tk = lax.broadcasted_iota(jnp.int32, (n_rows, pad), 1)
        causal = tk < tq
        sp = _softplus2(zz)
        fail = jnp.where(causal, sp, 0.0)
        rest_new = _dot(fail, m_new_ref[...], HIGHEST)
        wgt = jnp.where(causal, jnp.exp2(zz - sp + rest_new), 0.0)
        acc_scr[...] = _dot(wgt.astype(BF16), vb)
        carry_scr[...] = jnp.broadcast_to(-jnp.sum(fail, axis=-1, keepdims=True), carry_scr.shape)

    zs = [_dot(q_bd, k_refs[i][0].astype(BF16)) + bias_col for i in range(pages_per_step)]
    sps = [_softplus2(zz) for zz in zs]
    sums = _dot(jnp.concatenate([sp.astype(BF16) for sp in sps], axis=0), m_ref[...])
    acc = acc_scr[...]
    carry = carry_scr[...]
    for i in range(pages_per_step):
        s_i = sums[i * n_rows:(i + 1) * n_rows]
        rest_p = s_i[:, :page] + pltpu.repeat(carry, page // LANES, axis=1)
        wgt = jnp.exp2(zs[i] - sps[i] + rest_p)
        acc = acc + _dot_nt(wgt.astype(BF16), v_refs[i][0].astype(BF16))
        carry = carry + s_i[:, page:]
    acc_scr[...] = acc
    carry_scr[...] = carry

    @pl.when(step == n_steps - 1)
    def _():
        for t in range(n_new):
            blk = jnp.where(head_mask, acc[t * n_heads:(t + 1) * n_heads, :], 0.0)
            o_ref[0, t:t + 1, :] = jnp.sum(blk, axis=0, keepdims=True)


def _sb_kernel(qi_ref, kj_ref, bias_ref, pt_ref, q_ref, k_ref, v_ref, m_ref,
               qd_ref, kn_ref, vn_ref, md_ref, m_new_ref, *rest,
               n_pairs, n_dec, dec_steps, pages_per_step, blk, n_sub, n_heads, head_dim, n_new):
    k_refs = rest[:pages_per_step]
    v_refs = rest[pages_per_step:2 * pages_per_step]
    o_ref, od_ref, carry_scr, acc_scr, carry_d_scr, acc_d_scr = rest[2 * pages_per_step:]
    step = pl.program_id(0)

    @pl.when(step < n_pairs)
    def _():
        _sb_prompt_step(qi_ref[step], kj_ref[step], bias_ref, q_ref, k_ref, v_ref, m_ref, o_ref,
                        carry_scr, acc_scr, blk=blk, n_sub=n_sub, n_heads=n_heads, head_dim=head_dim)

    @pl.when(step < n_dec)
    def _():
        _sb_decode_step(step % dec_steps, dec_steps, bias_ref, qd_ref, kn_ref, vn_ref, md_ref, m_new_ref,
                        k_refs, v_refs, od_ref, carry_d_scr, acc_d_scr,
                        n_heads=n_heads, head_dim=head_dim, n_new=n_new)


def _sb_attention(q, k, v, q_dec, k_new, v_new, cache_kt, cache_vt, page_table, bias,
                  *, blk, n_sub, pages_per_step, n_heads):
    width, rows = v.shape
    head_dim = width // n_heads
    nq = rows // (blk * n_sub)
    pairs = [(i, j) for i in range(nq) for j in range(i * n_sub + n_sub - 1, -1, -1)]
    n_pairs = len(pairs)
    n_seq, n_new, _ = q_dec.shape
    n_pages = page_table.shape[1]
    page = cache_kt.shape[2]
    dec_steps = n_pages // pages_per_step
    n_dec = n_seq * dec_steps
    n_steps = max(n_pairs, n_dec)
    pairs = pairs + [pairs[-1]] * (n_steps - n_pairs)
    qi = jnp.asarray([p[0] for p in pairs], jnp.int32)
    kj = jnp.asarray([p[1] for p in pairs], jnp.int32)
    pad = k_new.shape[1]
    n_rows = n_new * n_heads
    excl = jnp.asarray(_suffix_matrix(blk, False), BF16)
    excl_dec = jnp.asarray(_suffix_matrix(page, True), BF16)
    excl_new = jnp.asarray(-np.tril(np.ones((pad, pad), np.float32), -1), F32)

    def dec_pos(s):
        sd = jnp.minimum(s, n_dec - 1)
        return sd // dec_steps, sd % dec_steps

    def page_spec(i):
        def index_map(s, qi_, kj_, bias_, pt):
            b, ds = dec_pos(s)
            logical = n_pages - 1 - (ds * pages_per_step + i)
            return (pt[b * n_pages + logical], 0, 0)
        return pl.BlockSpec((1, width, page), index_map)

    seq_spec = lambda r: pl.BlockSpec((1, r, width), lambda s, qi_, kj_, bias_, pt: (dec_pos(s)[0], 0, 0))
    const = lambda a: pl.BlockSpec(a.shape, lambda s, qi_, kj_, bias_, pt: (0, 0))
    grid_spec = pltpu.PrefetchScalarGridSpec(
        num_scalar_prefetch=4,
        grid=(n_steps,),
        in_specs=[pl.BlockSpec((blk * n_sub, 2 * width), lambda s, qi_, kj_, bias_, pt: (qi_[s], 0)),
                  pl.BlockSpec((blk, 2 * width), lambda s, qi_, kj_, bias_, pt: (kj_[s], 0)),
                  pl.BlockSpec((width, blk), lambda s, qi_, kj_, bias_, pt: (0, kj_[s])),
                  const(excl),
                  seq_spec(n_new), seq_spec(pad), seq_spec(pad), const(excl_dec), const(excl_new)]
                 + [page_spec(i) for i in range(pages_per_step)] * 2,
        out_specs=[pl.BlockSpec((blk * n_sub, width), lambda s, qi_, kj_, bias_, pt: (qi_[s], 0)),
                   seq_spec(n_new)],
        scratch_shapes=[pltpu.VMEM((n_sub, n_heads, BF16_ROWS, blk), F32),
                        pltpu.VMEM((n_sub, n_heads, head_dim, blk), F32),
                        pltpu.VMEM((n_rows, LANES), F32), pltpu.VMEM((n_rows, width), F32)],
    )
    return pl.pallas_call(
        functools.partial(_sb_kernel, n_pairs=n_pairs, n_dec=n_dec, dec_steps=dec_steps,
                          pages_per_step=pages_per_step, blk=blk, n_sub=n_sub, n_heads=n_heads,
                          head_dim=head_dim, n_new=n_new),
        grid_spec=grid_spec,
        out_shape=[jax.ShapeDtypeStruct((rows, width), F32),
                   jax.ShapeDtypeStruct((n_seq, n_new, width), F32)],
        compiler_params=pltpu.CompilerParams(
            dimension_semantics=("arbitrary",), vmem_limit_bytes=VMEM_LIMIT),
        name="sb_attention",
    )(qi, kj, bias, page_table.reshape(-1), q, k, v, excl, q_dec, k_new, v_new, excl_dec, excl_new,
      *([cache_kt] * pages_per_step), *([cache_vt] * pages_per_step))


def _merge_kernel(x_ref, orw_ref, osb_ref, zsb_ref, grw_ref, gsb_ref, gate_ref, gain_ref,
                  wrw_ref, wsb_ref, wout_ref, y_ref):
    o_sb = osb_ref[...] * _silu(zsb_ref[...])
    m = (_sigmoid(grw_ref[...]) * _dot(orw_ref[...].astype(BF16), wrw_ref[...])
         + _sigmoid(gsb_ref[...]) * _dot(o_sb.astype(BF16), wsb_ref[...]))
    u = _dot(m.astype(BF16), wout_ref[...])
    ms = jnp.mean(u * u, axis=-1, keepdims=True)
    y_ref[...] = x_ref[...] + gate_ref[...] * (u * lax.rsqrt(ms + RMS_EPS) * gain_ref[...])


def _merge(x, o_rw, o_sb, z_sb, g_rw, g_sb, gate, gain, w_rw, w_sb, w_out, tm):
    rows, d = x.shape
    width = o_rw.shape[1]
    per_row = gate.shape[0] != 1
    gate_spec = (pl.BlockSpec((tm, d), lambda i: (i, 0)) if per_row
                 else pl.BlockSpec((1, d), lambda i: (0, 0)))
    row_spec = lambda w: pl.BlockSpec((tm, w), lambda i: (i, 0))
    const = lambda arr: pl.BlockSpec(arr.shape, lambda i: (0, 0))
    return pl.pallas_call(
        _merge_kernel,
        grid=(rows // tm,),
        in_specs=[row_spec(d), row_spec(width), row_spec(width), row_spec(width),
                  row_spec(d), row_spec(d), gate_spec, const(gain),
                  const(w_rw), const(w_sb), const(w_out)],
        out_specs=row_spec(d),
        out_shape=jax.ShapeDtypeStruct((rows, d), F32),
        compiler_params=pltpu.CompilerParams(
            dimension_semantics=("parallel",), vmem_limit_bytes=VMEM_LIMIT),
        name="merge_out",
    )(x, o_rw, o_sb, z_sb, g_rw, g_sb, gate, gain, w_rw, w_sb, w_out)


def _row_tile(rows, want):
    t = min(rows, want)
    while rows % t:
        t //= 2
    return t


def _tiles(seq, rows_sample, n_seq, n_pages):
    sb_blk = _row_tile(seq, 256)
    sb_sub = 2 if seq % (2 * sb_blk) == 0 else 1
    n_q = seq // (sb_blk * sb_sub)
    n_pairs = sum(i * sb_sub + sb_sub for i in range(n_q))
    options = [p for p in (1, 2, 4, 8, 16) if n_pages % p == 0]
    fitting = [p for p in options if n_seq * n_pages // p <= n_pairs]
    return dict(
        proj_rows=_row_tile(seq, 256), proj_rows_sample=_row_tile(rows_sample, 256),
        merge_rows=_row_tile(seq, 512), merge_rows_sample=_row_tile(rows_sample, 512),
        rwkv_chunk=_row_tile(seq, 128), rwkv_group=_row_tile(n_seq, 4),
        sb_blk=sb_blk, sb_sub=sb_sub,
        pages_per_step=fitting[0] if fitting else options[-1])


def kernel(x_prompt, x_sample, cache_sb_k, cache_sb_v, state_rwkv_wkv, state_rwkv_shift, page_table, c_prompt, c_sample, w_ada, b_ada, norm_pre, norm_post, w_in, mu_shift, w0_decay, w_decay_up, a0, w_a_up, k_k, k_a, r_k, ln_x_w, ln_x_b, sb_bias, w_branch_rwkv, w_branch_sb, w_out):
    bp, seq, d = x_prompt.shape
    bs, t_new, _ = x_sample.shape
    rw_heads, rw_dim = r_k.shape
    rw_width = rw_heads * rw_dim
    rw_cols = mu_shift.shape[0]
    n_pool, page, sb_heads, sb_dim = cache_sb_k.shape
    sb_width = sb_heads * sb_dim
    widths = (rw_cols, sb_width, sb_width, sb_width, sb_width, d, d)
    assert sum(widths) == w_in.shape[1]
    q_scale = float(sb_dim) ** -0.5 * LOG2_E
    bias2 = sb_bias * LOG2_E
    pieces, left = [], bias2
    for _ in range(AUG):
        pieces.append(left.astype(BF16).astype(F32))
        left = left - pieces[-1]
    bias_aug = jnp.pad(jnp.stack(pieces, axis=1), ((0, 0), (0, sb_dim - AUG)))

    n_c = bp + bs
    c_all = jnp.concatenate([c_prompt, c_sample], axis=0)
    c_all = jnp.pad(c_all, ((0, (-n_c) % SUBLANES), (0, 0)))
    mod = _modulation(c_all, w_ada, b_ada)
    shift_all, scale_all, gate_all = mod[:, :d], mod[:, d:2 * d], mod[:, 2 * d:]

    w_in_bf = w_in.astype(BF16)
    w_rw_bf = w_branch_rwkv.astype(BF16)
    w_sb_bf = w_branch_sb.astype(BF16)
    w_out_bf = w_out.astype(BF16)
    gain_pre = norm_pre.reshape(1, d)
    gain_post = norm_post.reshape(1, d)
    row2 = lambda a: a.reshape(1, -1)
    rw_params = (row2(mu_shift), row2(w0_decay), w_decay_up, row2(a0), w_a_up, row2(k_k),
                 row2(k_a), row2(r_k), row2(ln_x_w), row2(ln_x_b))
    head_of = np.arange(rw_width) // rw_dim
    bd = jnp.asarray(head_of[:, None] == head_of[None, :], BF16)

    assert bp == 1, "one prompt sequence: its sweep shares a pallas_call with the paged decode"
    rows_s = bs * t_new
    n_pages = page_table.shape[1]
    tiles = _tiles(seq, rows_s, bs, n_pages)

    x = x_prompt[0]
    p_rw, _, k, v, z, g_rw, g_sb, q_aug, k_aug, v_bf = _project(
        x, scale_all[0:1], shift_all[0:1], gain_pre, w_in_bf, bias_aug, widths, q_scale,
        tiles["proj_rows"])
    o_rw, wkv_p = _rwkv(p_rw, jnp.zeros((1, 1, rw_cols), F32),
                        jnp.zeros((1, rw_heads, rw_dim, rw_dim), F32), rw_params, bd,
                        n_seq=1, chunk=tiles["rwkv_chunk"], t_valid=tiles["rwkv_chunk"], group=1)

    xs = x_sample.reshape(rows_s, d)
    expand = lambda a: jnp.repeat(a[bp:bp + bs], t_new, axis=0)
    p_rw_s, q_s, k_s, v_s, z_s, g_rw_s, g_sb_s, _, _, _ = _project(
        xs, expand(scale_all), expand(shift_all), gain_pre, w_in_bf, bias_aug, widths, q_scale,
        tiles["proj_rows_sample"])
    t_pad = -(-t_new // BF16_ROWS) * BF16_ROWS
    pad_rows = lambda a: jnp.pad(a.reshape(bs, t_new, -1), ((0, 0), (0, t_pad - t_new), (0, 0)))
    o_rw_s, wkv_s = _rwkv(pad_rows(p_rw_s).reshape(bs * t_pad, rw_cols),
                          state_rwkv_shift.reshape(bs, 1, rw_cols), state_rwkv_wkv, rw_params, bd,
                          n_seq=bs, chunk=t_pad, t_valid=t_new, group=tiles["rwkv_group"])
    o_rw_s = o_rw_s.reshape(bs, t_pad, rw_width)[:, :t_new].reshape(rows_s, rw_width)

    to_kt = lambda c: jnp.transpose(c, (0, 2, 3, 1)).reshape(n_pool, sb_width, page)
    o_sb, o_sb_s = _sb_attention(
        q_aug, k_aug, v_bf.T, q_s.reshape(bs, t_new, sb_width), pad_rows(k_s), pad_rows(v_s),
        to_kt(cache_sb_k), to_kt(cache_sb_v), page_table, bias2,
        blk=tiles["sb_blk"], n_sub=tiles["sb_sub"], pages_per_step=tiles["pages_per_step"],
        n_heads=sb_heads)

    y_p = _merge(x, o_rw, o_sb, z, g_rw, g_sb, gate_all[0:1], gain_post,
                 w_rw_bf, w_sb_bf, w_out_bf, tiles["merge_rows"])
    y_s = _merge(xs, o_rw_s, o_sb_s.reshape(rows_s, sb_width), z_s, g_rw_s, g_sb_s, expand(gate_all),
                 gain_post, w_rw_bf, w_sb_bf, w_out_bf, tiles["merge_rows_sample"])
    return (y_p[None], y_s.reshape(bs, t_new, d),
            k.reshape(bp, seq, sb_heads, sb_dim), v.reshape(bp, seq, sb_heads, sb_dim),
            k_s.reshape(bs, t_new, sb_heads, sb_dim), v_s.reshape(bs, t_new, sb_heads, sb_dim),
            wkv_p, wkv_s, p_rw[seq - 1:seq], p_rw_s.reshape(bs, t_new, rw_cols)[:, -1])
```
